```python
import math
import jax, jax.numpy as jnp
from jax import lax
import numpy as np

D_MODEL = 1024
BATCH = 16
SEQ = 2048
DEPTH = 1

MEM_LEN = 256
MAX_POS_OFFSET = 1024
ML_HEADS = 4
ML_DQK = 128
ML_DV = 256
ML_CHUNK = 64
ML_CONV = 4
DA_HEADS = 8
DA_DHEAD = 64
DA_DV = 2 * DA_DHEAD
Q_BLOCK = 128
ROPE_THETA = 10000.0
XA_HEADS = 4
XA_DHEAD = D_MODEL // XA_HEADS
FFN_HIDDEN = -(-8 * D_MODEL // (3 * 256)) * 256
ALPHA = (2.0 * DEPTH) ** 0.25
BETA = (8.0 * DEPTH) ** -0.25
LN_EPS = 1e-5

ML_QK_W = 2 * ML_HEADS * ML_DQK
ML_V_W = ML_HEADS * ML_DV
ML_O_W = ML_HEADS * ML_DV
ML_IF_W = 2 * ML_HEADS
DA_Q_W = DA_HEADS * 2 * DA_DHEAD
DA_K_W = DA_HEADS * 2 * DA_DHEAD
DA_V_W = DA_HEADS * DA_DV
GATE_W = 2 * D_MODEL
IN_SIZES = (ML_QK_W, ML_V_W, ML_O_W, ML_IF_W, DA_Q_W, DA_K_W, DA_V_W, GATE_W)
IN_DIM = ML_QK_W + ML_V_W + ML_O_W + ML_IF_W + DA_Q_W + DA_K_W + DA_V_W + GATE_W
F_GATE_OFFSET = ML_QK_W + ML_V_W + ML_O_W + ML_HEADS

kernel_name = 'hybrid_mlstm_diffattn_gated_deepnorm'


def layer_norm(x, g, b):
    xf = x.astype(jnp.float32)
    mu = xf.mean(-1, keepdims=True)
    var = jnp.square(xf - mu).mean(-1, keepdims=True)
    return ((xf - mu) * lax.rsqrt(var + LN_EPS) * g + b).astype(x.dtype)


def head_layer_norm(h, g):
    mu = h.mean(-1, keepdims=True)
    var = jnp.square(h - mu).mean(-1, keepdims=True)
    return (h - mu) * lax.rsqrt(var + LN_EPS) * g


def head_rms_norm(h, g):
    hf = h.astype(jnp.float32)
    return hf * lax.rsqrt(jnp.square(hf).mean(-1, keepdims=True) + LN_EPS) * g


def causal_dwconv(x, w, b):
    K, C = w.shape
    y = lax.conv_general_dilated(x, w[:, None, :], window_strides=(1,), padding=((K - 1, 0),),
                                 dimension_numbers=('NWC', 'WIO', 'NWC'), feature_group_count=C)
    return y + b


def rope(x, pos):
    half = x.shape[-1] // 2
    inv = ROPE_THETA ** (-jnp.arange(half, dtype=jnp.float32) / half)
    ang = pos.astype(jnp.float32)[..., None] * inv
    cos = jnp.cos(ang)[:, :, None, :]
    sin = jnp.sin(ang)[:, :, None, :]
    x1, x2 = x[..., :half], x[..., half:]
    return jnp.concatenate([x1 * cos - x2 * sin, x2 * cos + x1 * sin], axis=-1).astype(x.dtype)


def mlstm_chunkwise(q, k, v, i_pre, f_pre):
    B, S, H, dk = q.shape
    dv = v.shape[-1]
    L = ML_CHUNK
    nc = S // L
    f32 = jnp.float32

    def chunks(t):
        return t.astype(f32).reshape(B, nc, L, H, t.shape[-1]).transpose(1, 0, 3, 2, 4)

    def gchunks(t):
        return t.astype(f32).reshape(B, nc, L, H).transpose(1, 0, 3, 2)

    qc = chunks(q) * (dk ** -0.5)
    kc, vc = chunks(k), chunks(v)
    ic = gchunks(i_pre)
    lfc = jax.nn.log_sigmoid(gchunks(f_pre))
    causal = jnp.tril(jnp.ones((L, L), dtype=bool))

    def step(carry, inp):
        C, n, m = carry
        qb, kb, vb, ib, lfb = inp
        bcum = jnp.cumsum(lfb, axis=-1)
        dmat = bcum[..., :, None] - bcum[..., None, :] + ib[..., None, :]
        dmat = jnp.where(causal, dmat, -jnp.inf)
        inter = bcum + m[..., None]
        m_row = jnp.maximum(inter, dmat.max(-1))
        w_inter = jnp.exp(inter - m_row)
        w_intra = jnp.exp(dmat - m_row[..., None])
        qk = jnp.einsum('bhtd,bhsd->bhts', qb, kb) * w_intra
        num = (w_inter[..., None] * jnp.einsum('bhtd,bhdv->bhtv', qb, C)
               + jnp.einsum('bhts,bhsv->bhtv', qk, vb))
        den = w_inter * jnp.einsum('bhtd,bhd->bht', qb, n) + qk.sum(-1)
        h = num / jnp.maximum(jnp.abs(den), jnp.exp(-m_row))[..., None]
        b_last = bcum[..., -1]
        g = b_last[..., None] - bcum + ib
        m_new = jnp.maximum(b_last + m, g.max(-1))
        decay = jnp.exp(b_last + m - m_new)
        ws = jnp.exp(g - m_new[..., None])
        C_new = decay[..., None, None] * C + jnp.einsum('bhs,bhsd,bhsv->bhdv', ws, kb, vb)
        n_new = decay[..., None] * n + jnp.einsum('bhs,bhsd->bhd', ws, kb)
        return (C_new, n_new, m_new), h

    init = (jnp.zeros((B, H, dk, dv), f32), jnp.zeros((B, H, dk), f32), jnp.zeros((B, H), f32))
    _, hs = lax.scan(step, init, (qc, kc, vc, ic, lfc))
    return hs.transpose(1, 0, 3, 2, 4).reshape(B, S, H, dv)


def diff_attention(q, k, v, lam):
    B, S, H, _, d = q.shape
    dv = v.shape[-1]
    nb = S // Q_BLOCK
    qb = q.reshape(B, nb, Q_BLOCK, H, 2, d).transpose(1, 0, 3, 4, 2, 5)
    kt = k.transpose(0, 2, 3, 1, 4)
    vt = v.transpose(0, 2, 1, 3)
    kpos = jnp.arange(S)
    scale = d ** -0.5

    def block(args):
        qi, i = args
        s = jnp.einsum('bhcqd,bhckd->bhcqk', qi, kt).astype(jnp.float32) * scale
        qpos = i * Q_BLOCK + jnp.arange(Q_BLOCK)
        s = jnp.where(kpos[None, :] <= qpos[:, None], s, -jnp.inf)
        p = jax.nn.softmax(s, axis=-1)
        a = p[:, :, 0] - lam * p[:, :, 1]
        return jnp.einsum('bhqk,bhkv->bhqv', a.astype(vt.dtype), vt)

    o = lax.map(block, (qb, jnp.arange(nb)))
    return o.transpose(1, 0, 3, 2, 4).reshape(B, S, H, dv)


def setup_inputs(seed: int = 0) -> dict:
    key = jax.random.key(seed)
    ks = jax.random.split(key, 32)

    def nrm(i, shape, scale):
        return jax.random.normal(ks[i], shape, jnp.float32) * scale

    D, F = D_MODEL, FFN_HIDDEN
    x = nrm(0, (BATCH, SEQ, D), 1.0)
    mem = nrm(1, (BATCH, MEM_LEN, D), 1.0)
    offs = jax.random.randint(ks[2], (BATCH, 1), 0, MAX_POS_OFFSET, dtype=jnp.int32)
    positions = offs + jnp.arange(SEQ, dtype=jnp.int32)[None, :]
    w_in = nrm(3, (DEPTH, D, IN_DIM), D ** -0.5)
    b_in = nrm(4, (DEPTH, IN_DIM), 0.02)
    b_in = b_in.at[:, F_GATE_OFFSET:F_GATE_OFFSET + ML_HEADS].add(jnp.linspace(3.0, 6.0, ML_HEADS))
    conv_w = nrm(5, (DEPTH, ML_CONV, ML_QK_W), ML_CONV ** -0.5)
    conv_b = nrm(6, (DEPTH, ML_QK_W), 0.02)
    ml_norm_w = 1.0 + nrm(7, (DEPTH, ML_HEADS * ML_DV), 0.02)
    lam_q1 = nrm(8, (DEPTH, DA_DHEAD), 0.1)
    lam_k1 = nrm(9, (DEPTH, DA_DHEAD), 0.1)
    lam_q2 = nrm(10, (DEPTH, DA_DHEAD), 0.1)
    lam_k2 = nrm(11, (DEPTH, DA_DHEAD), 0.1)
    da_norm_w = 1.0 + nrm(12, (DEPTH, DA_HEADS * DA_DV), 0.02)
    w_proj_a = nrm(13, (DEPTH, ML_V_W, D), ML_V_W ** -0.5)
    w_proj_b = nrm(14, (DEPTH, DA_V_W, D), DA_V_W ** -0.5)
    w_mix_out = nrm(15, (DEPTH, D, D), BETA * D ** -0.5)
    ln1_g = 1.0 + nrm(16, (DEPTH, D), 0.02)
    ln1_b = nrm(17, (DEPTH, D), 0.02)
    w_xq = nrm(18, (DEPTH, D, XA_HEADS * XA_DHEAD), D ** -0.5)
    w_xk = nrm(19, (DEPTH, D, XA_HEADS * XA_DHEAD), D ** -0.5)
    w_xv = nrm(20, (DEPTH, D, XA_HEADS * XA_DHEAD), D ** -0.5)
    w_xo = nrm(21, (DEPTH, XA_HEADS * XA_DHEAD, D), BETA * (XA_HEADS * XA_DHEAD) ** -0.5)
    ln2_g = 1.0 + nrm(22, (DEPTH, D), 0.02)
    ln2_b = nrm(23, (DEPTH, D), 0.02)
    w_ffn_gate = nrm(24, (DEPTH, D, F), D ** -0.5)
    w_ffn_up = nrm(25, (DEPTH, D, F), D ** -0.5)
    w_ffn_down = nrm(26, (DEPTH, F, D), BETA * F ** -0.5)
    ln3_g = 1.0 + nrm(27, (DEPTH, D), 0.02)
    ln3_b = nrm(28, (DEPTH, D), 0.02)
    return {'x': x, 'mem': mem, 'positions': positions, 'w_in': w_in, 'b_in': b_in,
            'conv_w': conv_w, 'conv_b': conv_b, 'ml_norm_w': ml_norm_w,
            'lam_q1': lam_q1, 'lam_k1': lam_k1, 'lam_q2': lam_q2, 'lam_k2': lam_k2,
            'da_norm_w': da_norm_w, 'w_proj_a': w_proj_a, 'w_proj_b': w_proj_b,
            'w_mix_out': w_mix_out, 'ln1_g': ln1_g, 'ln1_b': ln1_b,
            'w_xq': w_xq, 'w_xk': w_xk, 'w_xv': w_xv, 'w_xo': w_xo, 'ln2_g': ln2_g, 'ln2_b': ln2_b,
            'w_ffn_gate': w_ffn_gate, 'w_ffn_up': w_ffn_up, 'w_ffn_down': w_ffn_down,
            'ln3_g': ln3_g, 'ln3_b': ln3_b}


def reference(x, mem, positions, w_in, b_in, conv_w, conv_b, ml_norm_w, lam_q1, lam_k1, lam_q2,
              lam_k2, da_norm_w, w_proj_a, w_proj_b, w_mix_out, ln1_g, ln1_b, w_xq, w_xk, w_xv,
              w_xo, ln2_g, ln2_b, w_ffn_gate, w_ffn_up, w_ffn_down, ln3_g, ln3_b):
    B, S, D = x.shape
    M = mem.shape[1]
    offsets = [int(o) for o in np.cumsum(IN_SIZES)[:-1]]
    for l in range(DEPTH):
        proj = jnp.einsum('bsd,de->bse', x, w_in[l]) + b_in[l]
        ml_qk, ml_v, ml_o, ml_if, da_q, da_k, da_v, gates = jnp.split(proj, offsets, axis=-1)

        qk = jax.nn.silu(causal_dwconv(ml_qk, conv_w[l], conv_b[l]))
        mq, mk = jnp.split(qk, 2, axis=-1)
        h = mlstm_chunkwise(mq.reshape(B, S, ML_HEADS, ML_DQK), mk.reshape(B, S, ML_HEADS, ML_DQK),
                            ml_v.reshape(B, S, ML_HEADS, ML_DV),
                            ml_if[..., :ML_HEADS], ml_if[..., ML_HEADS:])
        h = head_layer_norm(h, ml_norm_w[l].reshape(ML_HEADS, ML_DV)).reshape(B, S, ML_V_W)
        y_a = (jax.nn.sigmoid(ml_o) * h).astype(x.dtype) @ w_proj_a[l]

        dq = rope(da_q.reshape(B, S, DA_HEADS * 2, DA_DHEAD), positions).reshape(B, S, DA_HEADS, 2, DA_DHEAD)
        dk = rope(da_k.reshape(B, S, DA_HEADS * 2, DA_DHEAD), positions).reshape(B, S, DA_HEADS, 2, DA_DHEAD)
        dvv = da_v.reshape(B, S, DA_HEADS, DA_DV)
        lam_init = 0.8 - 0.6 * math.exp(-0.3 * l)
        lam = (jnp.exp(jnp.sum(lam_q1[l] * lam_k1[l]).astype(jnp.float32))
               - jnp.exp(jnp.sum(lam_q2[l] * lam_k2[l]).astype(jnp.float32)) + lam_init)
        o = diff_attention(dq, dk, dvv, lam)
        o = head_rms_norm(o, da_norm_w[l].reshape(DA_HEADS, DA_DV)) * (1.0 - lam_init)
        y_b = o.reshape(B, S, DA_V_W).astype(x.dtype) @ w_proj_b[l]

        g_a, g_b = jnp.split(jax.nn.sigmoid(gates), 2, axis=-1)
        mix = (g_a * y_a + g_b * y_b) @ w_mix_out[l]
        x = layer_norm(ALPHA * x + mix, ln1_g[l], ln1_b[l])

        xq = (x @ w_xq[l]).reshape(B, S, XA_HEADS, XA_DHEAD)
        xk = (mem @ w_xk[l]).reshape(B, M, XA_HEADS, XA_DHEAD)
        xv = (mem @ w_xv[l]).reshape(B, M, XA_HEADS, XA_DHEAD)
        s = jnp.einsum('bqhd,bkhd->bhqk', xq, xk).astype(jnp.float32) * (XA_DHEAD ** -0.5)
        p = jax.nn.softmax(s, axis=-1)
        xo = jnp.einsum('bhqk,bkhd->bqhd', p.astype(xv.dtype), xv).reshape(B, S, XA_HEADS * XA_DHEAD)
        x = layer_norm(ALPHA * x + xo @ w_xo[l], ln2_g[l], ln2_b[l])

        hid = jax.nn.silu(x @ w_ffn_gate[l]) * (x @ w_ffn_up[l])
        x = layer_norm(ALPHA * x + hid @ w_ffn_down[l], ln3_g[l], ln3_b[l])
    return x
```

```python
import functools
import math

import numpy as np
import jax
import jax.numpy as jnp
from jax import lax
from jax.experimental import pallas as pl
from jax.experimental.pallas import tpu as pltpu

ML_HEADS = 4
ML_DQK = 128
ML_DV = 256
ML_CONV = 4
DA_HEADS = 8
DA_DHEAD = 64
DA_DV = 2 * DA_DHEAD
XA_HEADS = 4
ROPE_THETA = 10000.0
DEPTH = 1
ALPHA = (2.0 * DEPTH) ** 0.25
LN_EPS = 1e-5

V7X_LANES = 128
V7X_SUBLANES = 8
V7X_VMEM_LIMIT = 56 * 1024 * 1024

ROW_TILE = 512
ML_CHUNK = 256
ATT_TILE = 256

F32 = jnp.float32
BF16 = jnp.bfloat16


def _dot(a, b):
    return jnp.dot(a, b, preferred_element_type=F32)


def _dot_nt(a, b):
    return lax.dot_general(a, b, (((1,), (1,)), ((), ())), preferred_element_type=F32)


def _dot_tn(a, b):
    return lax.dot_general(a, b, (((0,), (0,)), ((), ())), preferred_element_type=F32)


def _layer_norm(z, g, b):
    mu = jnp.mean(z, axis=-1, keepdims=True)
    zc = z - mu
    var = jnp.mean(zc * zc, axis=-1, keepdims=True)
    return zc * lax.rsqrt(var + LN_EPS) * g + b


def _whole_vmem():
    return pl.BlockSpec(memory_space=pltpu.VMEM)


def _params(n_axes):
    return pltpu.CompilerParams(dimension_semantics=("arbitrary",) * n_axes,
                                vmem_limit_bytes=V7X_VMEM_LIMIT)


_C_QK, _C_V, _C_O, _C_DQ, _C_DK, _C_DV, _C_GATE, _C_IF = (
    0, 1024, 2048, 3072, 4096, 5120, 6144, 8192)
_W_PACKED = _C_IF + V7X_LANES


def _in_proj_kernel(x_ref, pos_ref, w_ref, b_ref, cw_ref, cb_ref, invf_ref,
                    qk_ref, v_ref, og_ref, dq_ref, dk_ref, dv_ref, gate_ref, gif_ref,
                    carry_ref, *, tiles_per_seq):
    tm = x_ref.shape[0]
    d = qk_ref.shape[1]
    xb = x_ref[...].astype(BF16)

    def proj(c0, width):
        return _dot(xb, w_ref[:, c0:c0 + width]) + b_ref[:, c0:c0 + width]

    @pl.when(pl.program_id(0) % tiles_per_seq == 0)
    def _():
        carry_ref[...] = jnp.zeros_like(carry_ref)

    p = proj(_C_QK, d)
    carry = carry_ref[...]
    row = lax.broadcasted_iota(jnp.int32, (V7X_SUBLANES, d), 0)
    acc = p * cw_ref[ML_CONV - 1:ML_CONV, :] + cb_ref[...]
    for k in range(1, ML_CONV):
        sh = pltpu.roll(p, k, 0)
        top = jnp.where(row < k, pltpu.roll(carry, k, 0), sh[0:V7X_SUBLANES, :])
        sh = jnp.concatenate([top, sh[V7X_SUBLANES:, :]], axis=0)
        acc = acc + sh * cw_ref[ML_CONV - 1 - k:ML_CONV - k, :]
    carry_ref[...] = p[tm - V7X_SUBLANES:tm, :]
    y = acc * jax.nn.sigmoid(acc)
    half = d // 2
    qk_ref[:, 0:half] = (y[:, 0:half] * (ML_DQK ** -0.5)).astype(BF16)
    qk_ref[:, half:d] = y[:, half:d].astype(BF16)

    v_ref[...] = proj(_C_V, d).astype(BF16)
    og_ref[...] = jax.nn.sigmoid(proj(_C_O, d)).astype(BF16)

    ang = pos_ref[...].astype(F32) * invf_ref[...]
    cosv = jnp.cos(ang)
    sinv = jnp.sin(ang)
    lane = lax.broadcasted_iota(jnp.int32, (1, V7X_LANES), 1)
    ssin = jnp.where(lane < V7X_LANES // 2, -sinv, sinv)

    def rope_store(out_ref, c0, scale):
        pr = proj(c0, d)
        for h in range(d // V7X_LANES):
            xh = pr[:, h * V7X_LANES:(h + 1) * V7X_LANES]
            r = xh * cosv + pltpu.roll(xh, V7X_LANES // 2, 1) * ssin
            out_ref[:, h * V7X_LANES:(h + 1) * V7X_LANES] = (r * scale).astype(BF16)

    rope_store(dq_ref, _C_DQ, DA_DHEAD ** -0.5)
    rope_store(dk_ref, _C_DK, 1.0)
    dv_ref[...] = proj(_C_DV, d).astype(BF16)

    gate_ref[:, 0:d] = jax.nn.sigmoid(proj(_C_GATE, d)).astype(BF16)
    gate_ref[:, d:2 * d] = jax.nn.sigmoid(proj(_C_GATE + d, d)).astype(BF16)
    gif_ref[...] = proj(_C_IF, V7X_LANES)


def _in_proj(x2, pos2, w_packed, b_packed, conv_w, conv_b, invf, seq):
    t, d = x2.shape
    tm = ROW_TILE
    row_blk = lambda w: pl.BlockSpec((tm, w), lambda i: (i, 0))
    outs = [jax.ShapeDtypeStruct((t, d), BF16)] * 6 + [
        jax.ShapeDtypeStruct((t, 2 * d), BF16), jax.ShapeDtypeStruct((t, V7X_LANES), F32)]
    return pl.pallas_call(
        functools.partial(_in_proj_kernel, tiles_per_seq=seq // tm),
        grid=(t // tm,),
        in_specs=[row_blk(d), row_blk(1), _whole_vmem(), _whole_vmem(), _whole_vmem(),
                  _whole_vmem(), _whole_vmem()],
        out_specs=[row_blk(d)] * 6 + [row_blk(2 * d), row_blk(V7X_LANES)],
        out_shape=outs,
        scratch_shapes=[pltpu.VMEM((V7X_SUBLANES, d), F32)],
        compiler_params=_params(1),
        name="in_proj",
    )(x2, pos2, w_packed, b_packed, conv_w, conv_b, invf)


def _split3_bf16(a):
    hi = a.astype(BF16)
    r1 = a - hi.astype(F32)
    mid = r1.astype(BF16)
    lo = (r1 - mid.astype(F32)).astype(BF16)
    return hi, mid, lo


def _mlstm_kernel(qk_ref, v_ref, og_ref, gif_ref, nw_ref, out_ref, c_ref, n_ref, m_ref):
    seq = qk_ref.shape[0]
    L = ML_CHUNK
    c_ref[...] = jnp.zeros_like(c_ref)
    n_ref[...] = jnp.zeros_like(n_ref)
    m_ref[...] = jnp.zeros_like(m_ref)

    ri = lax.broadcasted_iota(jnp.int32, (L, L), 0)
    ci = lax.broadcasted_iota(jnp.int32, (L, L), 1)
    causal = ci <= ri
    tri = jnp.where(causal, 1.0, 0.0).astype(BF16)

    def chunk(c, carry):
        r0 = pl.multiple_of(c * L, L)
        rows = pl.ds(r0, L)
        g = gif_ref[rows, :]
        lf = jax.nn.log_sigmoid(g)
        hi, mid, lo = _split3_bf16(lf)
        bc = _dot(tri, hi) + _dot(tri, mid) + _dot(tri, lo)
        z = g - pltpu.roll(bc, V7X_LANES - ML_HEADS, 1)
        zt = z.T
        for h in range(ML_HEADS):
            i_col = g[:, h:h + 1]
            b_col = bc[:, ML_HEADS + h:ML_HEADS + h + 1]
            r_row = zt[h:h + 1, :]
            m_prev = m_ref[h]
            dmat = jnp.where(causal, b_col + r_row, -jnp.inf)
            inter = b_col + m_prev
            m_row = jnp.maximum(inter, jnp.max(dmat, axis=1, keepdims=True))
            w_inter = jnp.exp(inter - m_row)
            w_intra = jnp.exp(dmat - m_row)
            q = qk_ref[rows, h * ML_DQK:(h + 1) * ML_DQK]
            k = qk_ref[rows, (ML_HEADS + h) * ML_DQK:(ML_HEADS + h + 1) * ML_DQK]
            v = v_ref[rows, h * ML_DV:(h + 1) * ML_DV]
            qk = _dot_nt(q, k) * w_intra
            cmat = c_ref[h]
            nvec = n_ref[h]
            num = w_inter * _dot(q, cmat.astype(BF16)) + _dot(qk.astype(BF16), v)
            den = (w_inter * jnp.sum(q.astype(F32) * nvec, axis=1, keepdims=True)
                   + jnp.sum(qk, axis=1, keepdims=True))
            hh = num / jnp.maximum(jnp.abs(den), jnp.exp(-m_row))
            b_last = b_col[L - 1:L, :]
            g_col = b_last - b_col + i_col
            m_new = jnp.maximum(b_last + m_prev, jnp.max(g_col, axis=0, keepdims=True))
            decay = jnp.exp(b_last + m_prev - m_new)
            ws = jnp.exp(g_col - m_new)
            kf = k.astype(F32)
            c_ref[h] = decay * cmat + _dot_tn(k, (v.astype(F32) * ws).astype(BF16))
            n_ref[h] = decay * nvec + jnp.sum(kf * ws, axis=0, keepdims=True)
            m_ref[h] = m_new
            mu = jnp.mean(hh, axis=1, keepdims=True)
            hc = hh - mu
            var = jnp.mean(hc * hc, axis=1, keepdims=True)
            hn = hc * lax.rsqrt(var + LN_EPS) * nw_ref[:, h * ML_DV:(h + 1) * ML_DV]
            gate = og_ref[rows, h * ML_DV:(h + 1) * ML_DV].astype(F32)
            out_ref[rows, h * ML_DV:(h + 1) * ML_DV] = (gate * hn).astype(BF16)
        return carry

    lax.fori_loop(0, seq // L, chunk, 0)


def _mlstm(qk, v, og, gif, ml_norm_w, batch, seq):
    t, d = v.shape
    seq_blk = lambda w: pl.BlockSpec((seq, w), lambda b: (b, 0))
    return pl.pallas_call(
        _mlstm_kernel,
        grid=(batch,),
        in_specs=[seq_blk(d), seq_blk(d), seq_blk(d), seq_blk(V7X_LANES), _whole_vmem()],
        out_specs=seq_blk(d),
        out_shape=jax.ShapeDtypeStruct((t, d), BF16),
        scratch_shapes=[pltpu.VMEM((ML_HEADS, ML_DQK, ML_DV), F32),
                        pltpu.VMEM((ML_HEADS, 1, ML_DQK), F32),
                        pltpu.VMEM((ML_HEADS, 1, 1), F32)],
        compiler_params=_params(1),
        name="mlstm",
    )(qk, v, og, gif, ml_norm_w)


def _diff_attn_kernel(lq1_ref, lk1_ref, lq2_ref, lk2_ref, q_ref, k_ref, v_ref, nw_ref,
                      o_ref, m_ref, l_ref, acc_ref, *, lam_init):
    tq = q_ref.shape[0]
    tk = ATT_TILE
    i = pl.program_id(2)
    lam = (jnp.exp(jnp.sum(lq1_ref[...] * lk1_ref[...], axis=1, keepdims=True))
           - jnp.exp(jnp.sum(lq2_ref[...] * lk2_ref[...], axis=1, keepdims=True)) + lam_init)

    q = q_ref[...]
    lane = lax.broadcasted_iota(jnp.int32, q.shape, 1)
    first = (lane & (DA_DHEAD // 2)) == 0
    zero = jnp.zeros_like(q)
    q12 = jnp.concatenate([jnp.where(first, q, zero), jnp.where(first, zero, q)], axis=0)

    m_ref[...] = jnp.full_like(m_ref, -jnp.inf)
    l_ref[...] = jnp.zeros_like(l_ref)
    acc_ref[...] = jnp.zeros_like(acc_ref)

    def tile(j, masked):
        rows = pl.ds(pl.multiple_of(j * tk, tk), tk)
        s = _dot_nt(q12, k_ref[rows, :])
        if masked:
            qi = lax.broadcasted_iota(jnp.int32, s.shape, 0) & (tq - 1)
            kj = lax.broadcasted_iota(jnp.int32, s.shape, 1)
            s = jnp.where(kj <= qi, s, -jnp.inf)
        m_prev = m_ref[...]
        m_new = jnp.maximum(m_prev, jnp.max(s, axis=1, keepdims=True))
        alpha = jnp.exp(m_prev - m_new)
        p = jnp.exp(s - m_new)
        l_ref[...] = alpha * l_ref[...] + jnp.sum(p, axis=1, keepdims=True)
        acc_ref[...] = alpha * acc_ref[...] + _dot(p.astype(BF16), v_ref[rows, :])
        m_ref[...] = m_new

    def body(j, carry):
        tile(j, False)
        return carry

    lax.fori_loop(0, i, body, 0)
    tile(i, True)

    acc = acc_ref[...]
    l = l_ref[...]
    o = acc[0:tq, :] / l[0:tq, :] - lam * (acc[tq:2 * tq, :] / l[tq:2 * tq, :])
    ms = jnp.mean(o * o, axis=1, keepdims=True)
    o_ref[...] = (o * lax.rsqrt(ms + LN_EPS) * nw_ref[...] * (1.0 - lam_init)).astype(BF16)


def _diff_attn(dq, dk, dv, lam_vecs, da_norm_w, batch, seq, lam_init):
    t, d = dq.shape
    tq = ATT_TILE
    assert tq == ATT_TILE and tq & (tq - 1) == 0
    nq = seq // tq
    lam_spec = pl.BlockSpec((1, DA_DHEAD), lambda b, h, i: (0, 0))
    q_spec = pl.BlockSpec((tq, DA_DV), lambda b, h, i: (b * nq + i, h))
    kv_spec = pl.BlockSpec((seq, DA_DV), lambda b, h, i: (b, h))
    return pl.pallas_call(
        functools.partial(_diff_attn_kernel, lam_init=lam_init),
        grid=(batch, DA_HEADS, nq),
        in_specs=[lam_spec] * 4 + [q_spec, kv_spec, kv_spec,
                                   pl.BlockSpec((1, DA_DV), lambda b, h, i: (0, h))],
        out_specs=q_spec,
        out_shape=jax.ShapeDtypeStruct((t, d), BF16),
        scratch_shapes=[pltpu.VMEM((2 * tq, 1), F32), pltpu.VMEM((2 * tq, 1), F32),
                        pltpu.VMEM((2 * tq, DA_DV), F32)],
        compiler_params=_params(3),
        name="diff_attn",
    )(*lam_vecs, dq, dk, dv, da_norm_w)


def _merge_kernel(x_ref, ha_ref, ob_ref, gate_ref, wa_ref, wb_ref, wm_ref, g_ref, b_ref, o_ref):
    d = x_ref.shape[1]
    ya = _dot(ha_ref[...], wa_ref[...])
    yb = _dot(ob_ref[...], wb_ref[...])
    mixin = gate_ref[:, 0:d].astype(F32) * ya + gate_ref[:, d:2 * d].astype(F32) * yb
    mix = _dot(mixin.astype(BF16), wm_ref[...])
    o_ref[...] = _layer_norm(ALPHA * x_ref[...] + mix, g_ref[...], b_ref[...])


def _merge(x2, ha, ob, gates, wa, wb, wm, g, b):
    t, d = x2.shape
    tm = ROW_TILE
    row_blk = lambda w: pl.BlockSpec((tm, w), lambda i: (i, 0))
    return pl.pallas_call(
        _merge_kernel,
        grid=(t // tm,),
        in_specs=[row_blk(d), row_blk(d), row_blk(d), row_blk(2 * d)] + [_whole_vmem()] * 5,
        out_specs=row_blk(d),
        out_shape=jax.ShapeDtypeStruct((t, d), F32),
        compiler_params=_params(1),
        name="merge_ln1",
    )(x2, ha, ob, gates, wa, wb, wm, g, b)


def _xattn_kv_kernel(mem_ref, wk_ref, wv_ref, k_ref, v_ref):
    mb = mem_ref[...].astype(BF16)
    k_ref[...] = _dot(mb, wk_ref[...]).astype(BF16)
    v_ref[...] = _dot(mb, wv_ref[...]).astype(BF16)


def _xattn_kv(mem2, wk, wv, mem_len):
    t, d = mem2.shape
    blk = pl.BlockSpec((mem_len, d), lambda b: (b, 0))
    return pl.pallas_call(
        _xattn_kv_kernel,
        grid=(t // mem_len,),
        in_specs=[blk, _whole_vmem(), _whole_vmem()],
        out_specs=[blk, blk],
        out_shape=[jax.ShapeDtypeStruct((t, d), BF16)] * 2,
        compiler_params=_params(1),
        name="xattn_kv",
    )(mem2, wk, wv)


def _xattn_kernel(x_ref, k_ref, v_ref, wq_ref, wo_ref, g_ref, b_ref, o_ref):
    d = x_ref.shape[1]
    dh = d // XA_HEADS
    x = x_ref[...]
    xq = (_dot(x.astype(BF16), wq_ref[...]) * (dh ** -0.5)).astype(BF16)
    heads = []
    for h in range(XA_HEADS):
        cols = slice(h * dh, (h + 1) * dh)
        s = _dot_nt(xq[:, cols], k_ref[:, cols])
        e = jnp.exp(s - jnp.max(s, axis=1, keepdims=True))
        p = e / jnp.sum(e, axis=1, keepdims=True)
        heads.append(_dot(p.astype(BF16), v_ref[:, cols]).astype(BF16))
    xo = _dot(jnp.concatenate(heads, axis=1), wo_ref[...])
    o_ref[...] = _layer_norm(ALPHA * x + xo, g_ref[...], b_ref[...])


def _xattn(x1, kx, vx, wq, wo, g, b, seq, mem_len):
    t, d = x1.shape
    tm = ROW_TILE
    tiles_per_seq = seq // tm
    row_blk = pl.BlockSpec((tm, d), lambda i: (i, 0))
    kv_blk = pl.BlockSpec((mem_len, d), lambda i: (i // tiles_per_seq, 0))
    return pl.pallas_call(
        _xattn_kernel,
        grid=(t // tm,),
        in_specs=[row_blk, kv_blk, kv_blk] + [_whole_vmem()] * 4,
        out_specs=row_blk,
        out_shape=jax.ShapeDtypeStruct((t, d), F32),
        compiler_params=_params(1),
        name="xattn_ln2",
    )(x1, kx, vx, wq, wo, g, b)


def _ffn_kernel(x_ref, wg_ref, wu_ref, wd_ref, g_ref, b_ref, o_ref):
    x = x_ref[...]
    xb = x.astype(BF16)
    gt = _dot(xb, wg_ref[...])
    up = _dot(xb, wu_ref[...])
    hid = (gt * jax.nn.sigmoid(gt) * up).astype(BF16)
    o_ref[...] = _layer_norm(ALPHA * x + _dot(hid, wd_ref[...]), g_ref[...], b_ref[...])


def _ffn(x2, wg, wu, wd, g, b):
    t, d = x2.shape
    tm = ROW_TILE
    row_blk = pl.BlockSpec((tm, d), lambda i: (i, 0))
    return pl.pallas_call(
        _ffn_kernel,
        grid=(t // tm,),
        in_specs=[row_blk] + [_whole_vmem()] * 5,
        out_specs=row_blk,
        out_shape=jax.ShapeDtypeStruct((t, d), F32),
        compiler_params=_params(1),
        name="ffn_ln3",
    )(x2, wg, wu, wd, g, b)


def _rope_lane_perm():
    half = DA_DHEAD // 2
    perm = []
    for h in range(DA_HEADS):
        base = h * DA_DV
        for part in range(2):
            for c in range(2):
                start = base + c * DA_DHEAD + part * half
                perm.extend(range(start, start + half))
    return np.asarray(perm, dtype=np.int32)


def kernel(x, mem, positions, w_in, b_in, conv_w, conv_b, ml_norm_w, lam_q1, lam_k1, lam_q2, lam_k2, da_norm_w, w_proj_a, w_proj_b, w_mix_out, ln1_g, ln1_b, w_xq, w_xk, w_xv, w_xo, ln2_g, ln2_b, w_ffn_gate, w_ffn_up, w_ffn_down, ln3_g, ln3_b):
    batch, seq, d = x.shape
    mem_len = mem.shape[1]
    depth = w_in.shape[0]
    assert depth == DEPTH and seq % ROW_TILE == 0 and seq % ML_CHUNK == 0 and seq % ATT_TILE == 0
    t = batch * seq

    ml_qk_w = 2 * ML_HEADS * ML_DQK
    ml_v_w = ML_HEADS * ML_DV
    da_w = DA_HEADS * DA_DV
    sizes = (ml_qk_w, ml_v_w, ml_v_w, 2 * ML_HEADS, da_w, da_w, da_w, 2 * d)
    offs = np.concatenate([[0], np.cumsum(sizes)])
    assert ml_qk_w == d and ml_v_w == d and da_w == d and offs[-1] == w_in.shape[2]
    perm = _rope_lane_perm()
    half = DA_DHEAD // 2
    invf = ROPE_THETA ** (-jnp.arange(half, dtype=F32) / half)
    invf = jnp.tile(invf, V7X_LANES // half).reshape(1, V7X_LANES)

    x2 = x.reshape(t, d)
    pos2 = positions.reshape(t, 1)
    mem2 = mem.reshape(batch * mem_len, d)

    for l in range(depth):
        def piece(a, j):
            return a[..., offs[j]:offs[j + 1]]

        def pack(a):
            pad = jnp.zeros(a.shape[:-1] + (V7X_LANES - 2 * ML_HEADS,), a.dtype)
            return jnp.concatenate(
                [piece(a, 0), piece(a, 1), piece(a, 2), piece(a, 4)[..., perm],
                 piece(a, 5)[..., perm], piece(a, 6), piece(a, 7), piece(a, 3), pad], axis=-1)

        w_packed = pack(w_in[l]).astype(BF16)
        b_packed = pack(b_in[l].reshape(1, -1))
        assert w_packed.shape[1] == _W_PACKED

        qk, mv, og, dq, dk, dv, gates, gif = _in_proj(
            x2, pos2, w_packed, b_packed, conv_w[l], conv_b[l].reshape(1, -1), invf, seq)

        ha = _mlstm(qk, mv, og, gif, ml_norm_w[l].reshape(1, -1), batch, seq)

        lam_init = 0.8 - 0.6 * math.exp(-0.3 * l)
        lam_vecs = [a[l].reshape(1, -1) for a in (lam_q1, lam_k1, lam_q2, lam_k2)]
        ob = _diff_attn(dq, dk, dv, lam_vecs, da_norm_w[l].reshape(1, -1), batch, seq, lam_init)

        x2 = _merge(x2, ha, ob, gates, w_proj_a[l].astype(BF16), w_proj_b[l].astype(BF16),
                    w_mix_out[l].astype(BF16), ln1_g[l].reshape(1, -1), ln1_b[l].reshape(1, -1))

        kx, vx = _xattn_kv(mem2, w_xk[l].astype(BF16), w_xv[l].astype(BF16), mem_len)
        x2 = _xattn(x2, kx, vx, w_xq[l].astype(BF16), w_xo[l].astype(BF16),
                    ln2_g[l].reshape(1, -1), ln2_b[l].reshape(1, -1), seq, mem_len)

        x2 = _ffn(x2, w_ffn_gate[l].astype(BF16), w_ffn_up[l].astype(BF16),
                  w_ffn_down[l].astype(BF16), ln3_g[l].reshape(1, -1), ln3_b[l].reshape(1, -1))

    return x2.reshape(batch, seq, d)
```

```python
import functools
import math

import numpy as np
import jax
import jax.numpy as jnp
from jax import lax
from jax.experimental import pallas as pl
from jax.experimental.pallas import tpu as pltpu

ML_HEADS = 4
ML_DQK = 128
ML_DV = 256
ML_CONV = 4
DA_HEADS = 8
DA_DHEAD = 64
DA_DV = 2 * DA_DHEAD
XA_HEADS = 4
ROPE_THETA = 10000.0
DEPTH = 1
ALPHA = (2.0 * DEPTH) ** 0.25
LN_EPS = 1e-5

V7X_LANES = 128
V7X_SUBLANES = 8
V7X_VMEM_LIMIT = 56 * 1024 * 1024

ROW_TILE = 512
ML_CHUNK = 256
ATT_TILE = 256
ATT_HEADS_PER_STEP = 4

F32 = jnp.float32
BF16 = jnp.bfloat16


def _dot(a, b):
    return jnp.dot(a, b, preferred_element_type=F32)


def _dot_nt(a, b):
    return lax.dot_general(a, b, (((1,), (1,)), ((), ())), preferred_element_type=F32)


def _dot_tn(a, b):
    return lax.dot_general(a, b, (((0,), (0,)), ((), ())), preferred_element_type=F32)


def _layer_norm(z, g, b):
    mu = jnp.mean(z, axis=-1, keepdims=True)
    zc = z - mu
    var = jnp.mean(zc * zc, axis=-1, keepdims=True)
    return zc * lax.rsqrt(var + LN_EPS) * g + b


def _whole_vmem():
    return pl.BlockSpec(memory_space=pltpu.VMEM)


def _params(n_axes):
    return pltpu.CompilerParams(dimension_semantics=("arbitrary",) * n_axes,
                                vmem_limit_bytes=V7X_VMEM_LIMIT)


_C_QK, _C_V, _C_O, _C_DQ, _C_DK, _C_GATE, _C_IF = (
    0, 1024, 2048, 3072, 4096, 5120, 7168)
_W_PACKED = _C_IF + V7X_LANES


def _in_proj_kernel(x_ref, pos_ref, w_ref, b_ref, cw_ref, cb_ref, invf_ref, wdvt_ref, bdvt_ref,
                    qk_ref, v_ref, og_ref, dq_ref, dk_ref, dvt_ref, gate_ref, gif_ref,
                    carry_ref, *, tiles_per_seq):
    tm = x_ref.shape[0]
    d = qk_ref.shape[1]
    xb = x_ref[...].astype(BF16)

    def proj(c0, width):
        return _dot(xb, w_ref[:, c0:c0 + width]) + b_ref[:, c0:c0 + width]

    @pl.when(pl.program_id(0) % tiles_per_seq == 0)
    def _():
        carry_ref[...] = jnp.zeros_like(carry_ref)

    p = proj(_C_QK, d)
    carry = carry_ref[...]
    row = lax.broadcasted_iota(jnp.int32, (V7X_SUBLANES, d), 0)
    acc = p * cw_ref[ML_CONV - 1:ML_CONV, :] + cb_ref[...]
    for k in range(1, ML_CONV):
        sh = pltpu.roll(p, k, 0)
        top = jnp.where(row < k, pltpu.roll(carry, k, 0), sh[0:V7X_SUBLANES, :])
        sh = jnp.concatenate([top, sh[V7X_SUBLANES:, :]], axis=0)
        acc = acc + sh * cw_ref[ML_CONV - 1 - k:ML_CONV - k, :]
    carry_ref[...] = p[tm - V7X_SUBLANES:tm, :]
    y = acc * jax.nn.sigmoid(acc)
    half = d // 2
    qk_ref[:, 0:half] = (y[:, 0:half] * (ML_DQK ** -0.5)).astype(BF16)
    qk_ref[:, half:d] = y[:, half:d].astype(BF16)

    v_ref[...] = proj(_C_V, d).astype(BF16)
    og_ref[...] = jax.nn.sigmoid(proj(_C_O, d)).astype(BF16)

    ang = pos_ref[...].astype(F32) * invf_ref[...]
    cosv = jnp.cos(ang)
    sinv = jnp.sin(ang)
    lane = lax.broadcasted_iota(jnp.int32, (1, V7X_LANES), 1)
    ssin = jnp.where(lane < V7X_LANES // 2, -sinv, sinv)

    def rope_store(out_ref, c0, scale):
        pr = proj(c0, d)
        for h in range(d // V7X_LANES):
            xh = pr[:, h * V7X_LANES:(h + 1) * V7X_LANES]
            r = xh * cosv + pltpu.roll(xh, V7X_LANES // 2, 1) * ssin
            out_ref[:, h * V7X_LANES:(h + 1) * V7X_LANES] = (r * scale).astype(BF16)

    rope_store(dq_ref, _C_DQ, DA_DHEAD ** -0.5)
    rope_store(dk_ref, _C_DK, 1.0)
    for c in range(tm // ATT_TILE):
        xc = xb[c * ATT_TILE:(c + 1) * ATT_TILE, :]
        dvt_ref[c] = (_dot_nt(wdvt_ref[...], xc) + bdvt_ref[...]).astype(BF16)

    gate_ref[:, 0:d] = jax.nn.sigmoid(proj(_C_GATE, d)).astype(BF16)
    gate_ref[:, d:2 * d] = jax.nn.sigmoid(proj(_C_GATE + d, d)).astype(BF16)
    gif_ref[...] = proj(_C_IF, V7X_LANES)


def _in_proj(x2, pos2, w_packed, b_packed, conv_w, conv_b, invf, w_dvt, b_dvt, seq):
    t, d = x2.shape
    tm = ROW_TILE
    kt = tm // ATT_TILE
    row_blk = lambda w: pl.BlockSpec((tm, w), lambda i: (i, 0))
    row_out = jax.ShapeDtypeStruct((t, d), BF16)
    outs = [row_out] * 5 + [jax.ShapeDtypeStruct((t // ATT_TILE, d, ATT_TILE), BF16),
                            jax.ShapeDtypeStruct((t, 2 * d), BF16),
                            jax.ShapeDtypeStruct((t, V7X_LANES), F32)]
    return pl.pallas_call(
        functools.partial(_in_proj_kernel, tiles_per_seq=seq // tm),
        grid=(t // tm,),
        in_specs=[row_blk(d), row_blk(1)] + [_whole_vmem()] * 7,
        out_specs=[row_blk(d)] * 5 + [pl.BlockSpec((kt, d, ATT_TILE), lambda i: (i, 0, 0)),
                                      row_blk(2 * d), row_blk(V7X_LANES)],
        out_shape=outs,
        scratch_shapes=[pltpu.VMEM((V7X_SUBLANES, d), F32)],
        compiler_params=_params(1),
        name="in_proj",
    )(x2, pos2, w_packed, b_packed, conv_w, conv_b, invf, w_dvt, b_dvt)


def _split3_bf16(a):
    hi = a.astype(BF16)
    r1 = a - hi.astype(F32)
    mid = r1.astype(BF16)
    lo = (r1 - mid.astype(F32)).astype(BF16)
    return hi, mid, lo


def _mlstm_kernel(qk_ref, v_ref, og_ref, gif_ref, nw_ref, out_ref, c_ref, n_ref, m_ref):
    seq = qk_ref.shape[0]
    L = ML_CHUNK
    c_ref[...] = jnp.zeros_like(c_ref)
    n_ref[...] = jnp.zeros_like(n_ref)
    m_ref[...] = jnp.zeros_like(m_ref)

    ri = lax.broadcasted_iota(jnp.int32, (L, L), 0)
    ci = lax.broadcasted_iota(jnp.int32, (L, L), 1)
    causal = ci <= ri
    tri = jnp.where(causal, 1.0, 0.0).astype(BF16)

    def chunk(c, carry):
        r0 = pl.multiple_of(c * L, L)
        rows = pl.ds(r0, L)
        g = gif_ref[rows, :]
        lf = jax.nn.log_sigmoid(g)
        hi, mid, lo = _split3_bf16(lf)
        bc = _dot(tri, hi) + _dot(tri, mid) + _dot(tri, lo)
        z = g - pltpu.roll(bc, V7X_LANES - ML_HEADS, 1)
        zt = z.T
        for h in range(ML_HEADS):
            i_col = g[:, h:h + 1]
            b_col = bc[:, ML_HEADS + h:ML_HEADS + h + 1]
            r_row = zt[h:h + 1, :]
            m_prev = m_ref[h]
            dmat = jnp.where(causal, b_col + r_row, -jnp.inf)
            inter = b_col + m_prev
            m_row = jnp.maximum(inter, jnp.max(dmat, axis=1, keepdims=True))
            w_inter = jnp.exp(inter - m_row)
            w_intra = jnp.exp(dmat - m_row)
            q = qk_ref[rows, h * ML_DQK:(h + 1) * ML_DQK]
            k = qk_ref[rows, (ML_HEADS + h) * ML_DQK:(ML_HEADS + h + 1) * ML_DQK]
            v = v_ref[rows, h * ML_DV:(h + 1) * ML_DV]
            qk = _dot_nt(q, k) * w_intra
            cmat = c_ref[h]
            nvec = n_ref[h]
            num = w_inter * _dot(q, cmat.astype(BF16)) + _dot(qk.astype(BF16), v)
            den = (w_inter * jnp.sum(q.astype(F32) * nvec, axis=1, keepdims=True)
                   + jnp.sum(qk, axis=1, keepdims=True))
            hh = num / jnp.maximum(jnp.abs(den), jnp.exp(-m_row))
            b_last = b_col[L - 1:L, :]
            g_col = b_last - b_col + i_col
            m_new = jnp.maximum(b_last + m_prev, jnp.max(g_col, axis=0, keepdims=True))
            decay = jnp.exp(b_last + m_prev - m_new)
            ws = jnp.exp(g_col - m_new)
            kf = k.astype(F32)
            c_ref[h] = decay * cmat + _dot_tn(k, (v.astype(F32) * ws).astype(BF16))
            n_ref[h] = decay * nvec + jnp.sum(kf * ws, axis=0, keepdims=True)
            m_ref[h] = m_new
            mu = jnp.mean(hh, axis=1, keepdims=True)
            hc = hh - mu
            var = jnp.mean(hc * hc, axis=1, keepdims=True)
            hn = hc * lax.rsqrt(var + LN_EPS) * nw_ref[:, h * ML_DV:(h + 1) * ML_DV]
            gate = og_ref[rows, h * ML_DV:(h + 1) * ML_DV].astype(F32)
            out_ref[rows, h * ML_DV:(h + 1) * ML_DV] = (gate * hn).astype(BF16)
        return carry

    lax.fori_loop(0, seq // L, chunk, 0)


def _mlstm(qk, v, og, gif, ml_norm_w, batch, seq):
    t, d = v.shape
    seq_blk = lambda w: pl.BlockSpec((seq, w), lambda b: (b, 0))
    return pl.pallas_call(
        _mlstm_kernel,
        grid=(batch,),
        in_specs=[seq_blk(d), seq_blk(d), seq_blk(d), seq_blk(V7X_LANES), _whole_vmem()],
        out_specs=seq_blk(d),
        out_shape=jax.ShapeDtypeStruct((t, d), BF16),
        scratch_shapes=[pltpu.VMEM((ML_HEADS, ML_DQK, ML_DV), F32),
                        pltpu.VMEM((ML_HEADS, 1, ML_DQK), F32),
                        pltpu.VMEM((ML_HEADS, 1, 1), F32)],
        compiler_params=_params(1),
        name="mlstm",
    )(qk, v, og, gif, ml_norm_w)


def _diff_attn_kernel(lq1_ref, lk1_ref, lq2_ref, lk2_ref, q_ref, k_ref, vt_ref, nw_ref,
                      o_ref, q12_ref, st_ref, m_ref, l_ref, acc_ref, *, lam_init):
    tq = q_ref.shape[0]
    tk = ATT_TILE
    heads = q_ref.shape[1] // DA_DV
    i = pl.program_id(2)
    lam = (jnp.exp(jnp.sum(lq1_ref[...] * lk1_ref[...], axis=1, keepdims=True))
           - jnp.exp(jnp.sum(lq2_ref[...] * lk2_ref[...], axis=1, keepdims=True)) + lam_init)

    lane = lax.broadcasted_iota(jnp.int32, (tq, DA_DV), 1)
    first = (lane & (DA_DHEAD // 2)) == 0
    for h in range(heads):
        q = q_ref[:, h * DA_DV:(h + 1) * DA_DV]
        zero = jnp.zeros_like(q)
        q12_ref[h, 0:tq, :] = jnp.where(first, q, zero)
        q12_ref[h, tq:2 * tq, :] = jnp.where(first, zero, q)

    m_ref[...] = jnp.full_like(m_ref, -jnp.inf)
    l_ref[...] = jnp.zeros_like(l_ref)
    acc_ref[...] = jnp.zeros_like(acc_ref)

    def scores(j, h):
        rows = pl.ds(pl.multiple_of(j * tk, tk), tk)
        st_ref[h] = _dot_nt(k_ref[rows, h * DA_DV:(h + 1) * DA_DV], q12_ref[h])

    def softmax(h, masked):
        st = st_ref[h]
        if masked:
            kj = lax.broadcasted_iota(jnp.int32, st.shape, 0)
            qi = lax.broadcasted_iota(jnp.int32, st.shape, 1) & (tq - 1)
            st = jnp.where(kj <= qi, st, -jnp.inf)
        m_prev = m_ref[h]
        m_new = jnp.maximum(m_prev, jnp.max(st, axis=0, keepdims=True))
        alpha = jnp.exp(m_prev - m_new)
        pt = jnp.exp(st - m_new)
        l_ref[h] = alpha * l_ref[h] + jnp.sum(pt, axis=0, keepdims=True)
        m_ref[h] = m_new
        return alpha, pt.astype(BF16)

    def accumulate(j, h, alpha, pt):
        acc_ref[h] = alpha * acc_ref[h] + _dot(vt_ref[j, h * DA_DV:(h + 1) * DA_DV, :], pt)

    for h in range(heads):
        scores(0, h)

    def body(j, carry):
        for h in range(heads):
            alpha, pt = softmax(h, False)
            scores(j + 1, h)
            accumulate(j, h, alpha, pt)
        return carry

    lax.fori_loop(0, i, body, 0)
    for h in range(heads):
        alpha, pt = softmax(h, True)
        accumulate(i, h, alpha, pt)

    for h in range(heads):
        acc = acc_ref[h]
        l = l_ref[h]
        ot = acc[:, 0:tq] / l[:, 0:tq] - lam * (acc[:, tq:2 * tq] / l[:, tq:2 * tq])
        o = ot.T
        ms = jnp.mean(o * o, axis=1, keepdims=True)
        o_ref[:, h * DA_DV:(h + 1) * DA_DV] = (
            o * lax.rsqrt(ms + LN_EPS) * nw_ref[:, h * DA_DV:(h + 1) * DA_DV]
            * (1.0 - lam_init)).astype(BF16)


def _diff_attn(dq, dk, dvt, lam_vecs, da_norm_w, batch, seq, lam_init):
    t, d = dq.shape
    tq = ATT_TILE
    assert tq & (tq - 1) == 0
    nq = seq // tq
    hb = ATT_HEADS_PER_STEP
    w = hb * DA_DV
    lam_spec = pl.BlockSpec((1, DA_DHEAD), lambda b, g, i: (0, 0))
    q_spec = pl.BlockSpec((tq, w), lambda b, g, i: (b * nq + i, g))
    return pl.pallas_call(
        functools.partial(_diff_attn_kernel, lam_init=lam_init),
        grid=(batch, DA_HEADS // hb, nq),
        in_specs=[lam_spec] * 4 + [q_spec,
                                   pl.BlockSpec((seq, w), lambda b, g, i: (b, g)),
                                   pl.BlockSpec((nq, w, tq), lambda b, g, i: (b, g, 0)),
                                   pl.BlockSpec((1, w), lambda b, g, i: (0, g))],
        out_specs=q_spec,
        out_shape=jax.ShapeDtypeStruct((t, d), BF16),
        scratch_shapes=[pltpu.VMEM((hb, 2 * tq, DA_DV), BF16),
                        pltpu.VMEM((hb, tq, 2 * tq), F32),
                        pltpu.VMEM((hb, 1, 2 * tq), F32), pltpu.VMEM((hb, 1, 2 * tq), F32),
                        pltpu.VMEM((hb, DA_DV, 2 * tq), F32)],
        compiler_params=_params(3),
        name="diff_attn",
    )(*lam_vecs, dq, dk, dvt, da_norm_w)


def _merge_kernel(x_ref, ha_ref, ob_ref, gate_ref, wa_ref, wb_ref, wm_ref, g_ref, b_ref, o_ref):
    d = x_ref.shape[1]
    ya = _dot(ha_ref[...], wa_ref[...])
    yb = _dot(ob_ref[...], wb_ref[...])
    mixin = gate_ref[:, 0:d].astype(F32) * ya + gate_ref[:, d:2 * d].astype(F32) * yb
    mix = _dot(mixin.astype(BF16), wm_ref[...])
    o_ref[...] = _layer_norm(ALPHA * x_ref[...] + mix, g_ref[...], b_ref[...])


def _merge(x2, ha, ob, gates, wa, wb, wm, g, b):
    t, d = x2.shape
    tm = ROW_TILE
    row_blk = lambda w: pl.BlockSpec((tm, w), lambda i: (i, 0))
    return pl.pallas_call(
        _merge_kernel,
        grid=(t // tm,),
        in_specs=[row_blk(d), row_blk(d), row_blk(d), row_blk(2 * d)] + [_whole_vmem()] * 5,
        out_specs=row_blk(d),
        out_shape=jax.ShapeDtypeStruct((t, d), F32),
        compiler_params=_params(1),
        name="merge_ln1",
    )(x2, ha, ob, gates, wa, wb, wm, g, b)


def _xattn_kv_kernel(mem_ref, wk_ref, wv_ref, k_ref, v_ref):
    mb = mem_ref[...].astype(BF16)
    k_ref[...] = _dot(mb, wk_ref[...]).astype(BF16)
    v_ref[...] = _dot(mb, wv_ref[...]).astype(BF16)


def _xattn_kv(mem2, wk, wv, mem_len):
    t, d = mem2.shape
    blk = pl.BlockSpec((mem_len, d), lambda b: (b, 0))
    return pl.pallas_call(
        _xattn_kv_kernel,
        grid=(t // mem_len,),
        in_specs=[blk, _whole_vmem(), _whole_vmem()],
        out_specs=[blk, blk],
        out_shape=[jax.ShapeDtypeStruct((t, d), BF16)] * 2,
        compiler_params=_params(1),
        name="xattn_kv",
    )(mem2, wk, wv)


def _xattn_kernel(x_ref, k_ref, v_ref, wq_ref, wo_ref, g_ref, b_ref, o_ref):
    d = x_ref.shape[1]
    dh = d // XA_HEADS
    x = x_ref[...]
    xq = (_dot(x.astype(BF16), wq_ref[...]) * (dh ** -0.5)).astype(BF16)
    heads = []
    for h in range(XA_HEADS):
        cols = slice(h * dh, (h + 1) * dh)
        s = _dot_nt(xq[:, cols], k_ref[:, cols])
        e = jnp.exp(s - jnp.max(s, axis=1, keepdims=True))
        p = e / jnp.sum(e, axis=1, keepdims=True)
        heads.append(_dot(p.astype(BF16), v_ref[:, cols]).astype(BF16))
    xo = _dot(jnp.concatenate(heads, axis=1), wo_ref[...])
    o_ref[...] = _layer_norm(ALPHA * x + xo, g_ref[...], b_ref[...])


def _xattn(x1, kx, vx, wq, wo, g, b, seq, mem_len):
    t, d = x1.shape
    tm = ROW_TILE
    tiles_per_seq = seq // tm
    row_blk = pl.BlockSpec((tm, d), lambda i: (i, 0))
    kv_blk = pl.BlockSpec((mem_len, d), lambda i: (i // tiles_per_seq, 0))
    return pl.pallas_call(
        _xattn_kernel,
        grid=(t // tm,),
        in_specs=[row_blk, kv_blk, kv_blk] + [_whole_vmem()] * 4,
        out_specs=row_blk,
        out_shape=jax.ShapeDtypeStruct((t, d), F32),
        compiler_params=_params(1),
        name="xattn_ln2",
    )(x1, kx, vx, wq, wo, g, b)


def _ffn_kernel(x_ref, wg_ref, wu_ref, wd_ref, g_ref, b_ref, o_ref):
    x = x_ref[...]
    xb = x.astype(BF16)
    gt = _dot(xb, wg_ref[...])
    up = _dot(xb, wu_ref[...])
    hid = (gt * jax.nn.sigmoid(gt) * up).astype(BF16)
    o_ref[...] = _layer_norm(ALPHA * x + _dot(hid, wd_ref[...]), g_ref[...], b_ref[...])


def _ffn(x2, wg, wu, wd, g, b):
    t, d = x2.shape
    tm = ROW_TILE
    row_blk = pl.BlockSpec((tm, d), lambda i: (i, 0))
    return pl.pallas_call(
        _ffn_kernel,
        grid=(t // tm,),
        in_specs=[row_blk] + [_whole_vmem()] * 5,
        out_specs=row_blk,
        out_shape=jax.ShapeDtypeStruct((t, d), F32),
        compiler_params=_params(1),
        name="ffn_ln3",
    )(x2, wg, wu, wd, g, b)


def _rope_lane_perm():
    half = DA_DHEAD // 2
    perm = []
    for h in range(DA_HEADS):
        base = h * DA_DV
        for part in range(2):
            for c in range(2):
                start = base + c * DA_DHEAD + part * half
                perm.extend(range(start, start + half))
    return np.asarray(perm, dtype=np.int32)


def kernel(x, mem, positions, w_in, b_in, conv_w, conv_b, ml_norm_w, lam_q1, lam_k1, lam_q2, lam_k2, da_norm_w, w_proj_a, w_proj_b, w_mix_out, ln1_g, ln1_b, w_xq, w_xk, w_xv, w_xo, ln2_g, ln2_b, w_ffn_gate, w_ffn_up, w_ffn_down, ln3_g, ln3_b):
    batch, seq, d = x.shape
    mem_len = mem.shape[1]
    depth = w_in.shape[0]
    assert depth == DEPTH and seq % ROW_TILE == 0 and seq % ML_CHUNK == 0 and seq % ATT_TILE == 0
    t = batch * seq

    ml_qk_w = 2 * ML_HEADS * ML_DQK
    ml_v_w = ML_HEADS * ML_DV
    da_w = DA_HEADS * DA_DV
    sizes = (ml_qk_w, ml_v_w, ml_v_w, 2 * ML_HEADS, da_w, da_w, da_w, 2 * d)
    offs = np.concatenate([[0], np.cumsum(sizes)])
    assert ml_qk_w == d and ml_v_w == d and da_w == d and offs[-1] == w_in.shape[2]
    perm = _rope_lane_perm()
    half = DA_DHEAD // 2
    invf = ROPE_THETA ** (-jnp.arange(half, dtype=F32) / half)
    invf = jnp.tile(invf, V7X_LANES // half).reshape(1, V7X_LANES)

    x2 = x.reshape(t, d)
    pos2 = positions.reshape(t, 1)
    mem2 = mem.reshape(batch * mem_len, d)

    for l in range(depth):
        def piece(a, j):
            return a[..., offs[j]:offs[j + 1]]

        def pack(a):
            pad = jnp.zeros(a.shape[:-1] + (V7X_LANES - 2 * ML_HEADS,), a.dtype)
            return jnp.concatenate(
                [piece(a, 0), piece(a, 1), piece(a, 2), piece(a, 4)[..., perm],
                 piece(a, 5)[..., perm], piece(a, 7), piece(a, 3), pad], axis=-1)

        w_packed = pack(w_in[l]).astype(BF16)
        b_packed = pack(b_in[l].reshape(1, -1))
        assert w_packed.shape[1] == _W_PACKED
        w_dvt = piece(w_in[l], 6).T.astype(BF16)
        b_dvt = piece(b_in[l], 6).reshape(-1, 1)

        qk, mv, og, dq, dk, dvt, gates, gif = _in_proj(
            x2, pos2, w_packed, b_packed, conv_w[l], conv_b[l].reshape(1, -1), invf,
            w_dvt, b_dvt, seq)

        ha = _mlstm(qk, mv, og, gif, ml_norm_w[l].reshape(1, -1), batch, seq)

        lam_init = 0.8 - 0.6 * math.exp(-0.3 * l)
        lam_vecs = [a[l].reshape(1, -1) for a in (lam_q1, lam_k1, lam_q2, lam_k2)]
        ob = _diff_attn(dq, dk, dvt, lam_vecs, da_norm_w[l].reshape(1, -1), batch, seq, lam_init)

        x2 = _merge(x2, ha, ob, gates, w_proj_a[l].astype(BF16), w_proj_b[l].astype(BF16),
                    w_mix_out[l].astype(BF16), ln1_g[l].reshape(1, -1), ln1_b[l].reshape(1, -1))

        kx, vx = _xattn_kv(mem2, w_xk[l].astype(BF16), w_xv[l].astype(BF16), mem_len)
        x2 = _xattn(x2, kx, vx, w_xq[l].astype(BF16), w_xo[l].astype(BF16),
                    ln2_g[l].reshape(1, -1), ln2_b[l].reshape(1, -1), seq, mem_len)

        x2 = _ffn(x2, w_ffn_gate[l].astype(BF16), w_ffn_up[l].astype(BF16),
                  w_ffn_down[l].astype(BF16), ln3_g[l].reshape(1, -1), ln3_b[l].reshape(1, -1))

    return x2.reshape(batch, seq, d)
```

```python
import functools
import math

import numpy as np
import jax
import jax.numpy as jnp
from jax import lax
from jax.experimental import pallas as pl
from jax.experimental.pallas import tpu as pltpu

ML_HEADS = 4
ML_DQK = 128
ML_DV = 256
ML_CONV = 4
DA_HEADS = 8
DA_DHEAD = 64
DA_DV = 2 * DA_DHEAD
XA_HEADS = 4
ROPE_THETA = 10000.0
DEPTH = 1
ALPHA = (2.0 * DEPTH) ** 0.25
LN_EPS = 1e-5

V7X_LANES = 128
V7X_SUBLANES = 8
V7X_VMEM_LIMIT = 56 * 1024 * 1024

ROW_TILE = 512
ML_CHUNK = 256
ATT_TILE = 256
ATT_HEADS_PER_STEP = 8
ATT_SCORE_LOOKAHEAD = 2
ATT_ONES_ROWS = 16
LOG2_E = math.log2(math.e)

F32 = jnp.float32
BF16 = jnp.bfloat16


def _dot(a, b):
    return jnp.dot(a, b, preferred_element_type=F32)


def _dot_nt(a, b):
    return lax.dot_general(a, b, (((1,), (1,)), ((), ())), preferred_element_type=F32)


def _dot_tn(a, b):
    return lax.dot_general(a, b, (((0,), (0,)), ((), ())), preferred_element_type=F32)


def _layer_norm(z, g, b):
    mu = jnp.mean(z, axis=-1, keepdims=True)
    zc = z - mu
    var = jnp.mean(zc * zc, axis=-1, keepdims=True)
    return zc * lax.rsqrt(var + LN_EPS) * g + b


def _whole_vmem():
    return pl.BlockSpec(memory_space=pltpu.VMEM)


def _params(n_axes):
    return pltpu.CompilerParams(dimension_semantics=("arbitrary",) * n_axes,
                                vmem_limit_bytes=V7X_VMEM_LIMIT)


_C_QK, _C_V, _C_O, _C_DQ, _C_DK, _C_GATE, _C_IF = (
    0, 1024, 2048, 3072, 4096, 5120, 7168)
_W_PACKED = _C_IF + V7X_LANES


def _in_proj_kernel(x_ref, pos_ref, w_ref, b_ref, cw_ref, cb_ref, invf_ref, wdvt_ref, bdvt_ref,
                    qk_ref, v_ref, og_ref, dq_ref, dk_ref, dvt_ref, gate_ref, gif_ref,
                    carry_ref, *, tiles_per_seq):
    tm = x_ref.shape[0]
    d = qk_ref.shape[1]
    xb = x_ref[...].astype(BF16)

    def proj(c0, width):
        return _dot(xb, w_ref[:, c0:c0 + width]) + b_ref[:, c0:c0 + width]

    @pl.when(pl.program_id(0) % tiles_per_seq == 0)
    def _():
        carry_ref[...] = jnp.zeros_like(carry_ref)

    p = proj(_C_QK, d)
    carry = carry_ref[...]
    row = lax.broadcasted_iota(jnp.int32, (V7X_SUBLANES, d), 0)
    acc = p * cw_ref[ML_CONV - 1:ML_CONV, :] + cb_ref[...]
    for k in range(1, ML_CONV):
        sh = pltpu.roll(p, k, 0)
        top = jnp.where(row < k, pltpu.roll(carry, k, 0), sh[0:V7X_SUBLANES, :])
        sh = jnp.concatenate([top, sh[V7X_SUBLANES:, :]], axis=0)
        acc = acc + sh * cw_ref[ML_CONV - 1 - k:ML_CONV - k, :]
    carry_ref[...] = p[tm - V7X_SUBLANES:tm, :]
    y = acc * jax.nn.sigmoid(acc)
    half = d // 2
    qk_ref[:, 0:half] = (y[:, 0:half] * (ML_DQK ** -0.5)).astype(BF16)
    qk_ref[:, half:d] = y[:, half:d].astype(BF16)

    v_ref[...] = proj(_C_V, d).astype(BF16)
    og_ref[...] = jax.nn.sigmoid(proj(_C_O, d)).astype(BF16)

    ang = pos_ref[...].astype(F32) * invf_ref[...]
    cosv = jnp.cos(ang)
    sinv = jnp.sin(ang)
    lane = lax.broadcasted_iota(jnp.int32, (1, V7X_LANES), 1)
    ssin = jnp.where(lane < V7X_LANES // 2, -sinv, sinv)

    def rope_store(out_ref, c0, scale):
        pr = proj(c0, d)
        for h in range(d // V7X_LANES):
            xh = pr[:, h * V7X_LANES:(h + 1) * V7X_LANES]
            r = xh * cosv + pltpu.roll(xh, V7X_LANES // 2, 1) * ssin
            out_ref[:, h * V7X_LANES:(h + 1) * V7X_LANES] = (r * scale).astype(BF16)

    rope_store(dq_ref, _C_DQ, DA_DHEAD ** -0.5 * LOG2_E)
    rope_store(dk_ref, _C_DK, 1.0)
    for c in range(tm // ATT_TILE):
        xc = xb[c * ATT_TILE:(c + 1) * ATT_TILE, :]
        dvt_ref[c] = (_dot_nt(wdvt_ref[...], xc) + bdvt_ref[...]).astype(BF16)

    gate_ref[:, 0:d] = jax.nn.sigmoid(proj(_C_GATE, d)).astype(BF16)
    gate_ref[:, d:2 * d] = jax.nn.sigmoid(proj(_C_GATE + d, d)).astype(BF16)
    gif_ref[...] = proj(_C_IF, V7X_LANES)


def _in_proj(x2, pos2, w_packed, b_packed, conv_w, conv_b, invf, w_dvt, b_dvt, seq):
    t, d = x2.shape
    tm = ROW_TILE
    kt = tm // ATT_TILE
    row_blk = lambda w: pl.BlockSpec((tm, w), lambda i: (i, 0))
    row_out = jax.ShapeDtypeStruct((t, d), BF16)
    dvt_rows = w_dvt.shape[0]
    outs = [row_out] * 5 + [jax.ShapeDtypeStruct((t // ATT_TILE, dvt_rows, ATT_TILE), BF16),
                            jax.ShapeDtypeStruct((t, 2 * d), BF16),
                            jax.ShapeDtypeStruct((t, V7X_LANES), F32)]
    return pl.pallas_call(
        functools.partial(_in_proj_kernel, tiles_per_seq=seq // tm),
        grid=(t // tm,),
        in_specs=[row_blk(d), row_blk(1)] + [_whole_vmem()] * 7,
        out_specs=[row_blk(d)] * 5 + [pl.BlockSpec((kt, dvt_rows, ATT_TILE), lambda i: (i, 0, 0)),
                                      row_blk(2 * d), row_blk(V7X_LANES)],
        out_shape=outs,
        scratch_shapes=[pltpu.VMEM((V7X_SUBLANES, d), F32)],
        compiler_params=_params(1),
        name="in_proj",
    )(x2, pos2, w_packed, b_packed, conv_w, conv_b, invf, w_dvt, b_dvt)


def _split3_bf16(a):
    hi = a.astype(BF16)
    r1 = a - hi.astype(F32)
    mid = r1.astype(BF16)
    lo = (r1 - mid.astype(F32)).astype(BF16)
    return hi, mid, lo


def _mlstm_kernel(qk_ref, v_ref, og_ref, gif_ref, nw_ref, out_ref, c_ref, n_ref, m_ref):
    seq = qk_ref.shape[0]
    L = ML_CHUNK
    c_ref[...] = jnp.zeros_like(c_ref)
    n_ref[...] = jnp.zeros_like(n_ref)
    m_ref[...] = jnp.zeros_like(m_ref)

    ri = lax.broadcasted_iota(jnp.int32, (L, L), 0)
    ci = lax.broadcasted_iota(jnp.int32, (L, L), 1)
    causal = ci <= ri
    tri = jnp.where(causal, 1.0, 0.0).astype(BF16)

    def chunk(c, carry):
        r0 = pl.multiple_of(c * L, L)
        rows = pl.ds(r0, L)
        g = gif_ref[rows, :]
        lf = jax.nn.log_sigmoid(g)
        hi, mid, lo = _split3_bf16(lf)
        bc = _dot(tri, hi) + _dot(tri, mid) + _dot(tri, lo)
        z = g - pltpu.roll(bc, V7X_LANES - ML_HEADS, 1)
        zt = z.T
        for h in range(ML_HEADS):
            i_col = g[:, h:h + 1]
            b_col = bc[:, ML_HEADS + h:ML_HEADS + h + 1]
            r_row = zt[h:h + 1, :]
            m_prev = m_ref[h]
            dmat = jnp.where(causal, b_col + r_row, -jnp.inf)
            inter = b_col + m_prev
            m_row = jnp.maximum(inter, jnp.max(dmat, axis=1, keepdims=True))
            w_inter = jnp.exp(inter - m_row)
            w_intra = jnp.exp(dmat - m_row)
            q = qk_ref[rows, h * ML_DQK:(h + 1) * ML_DQK]
            k = qk_ref[rows, (ML_HEADS + h) * ML_DQK:(ML_HEADS + h + 1) * ML_DQK]
            v = v_ref[rows, h * ML_DV:(h + 1) * ML_DV]
            qk = _dot_nt(q, k) * w_intra
            cmat = c_ref[h]
            nvec = n_ref[h]
            num = w_inter * _dot(q, cmat.astype(BF16)) + _dot(qk.astype(BF16), v)
            den = (w_inter * jnp.sum(q.astype(F32) * nvec, axis=1, keepdims=True)
                   + jnp.sum(qk, axis=1, keepdims=True))
            hh = num / jnp.maximum(jnp.abs(den), jnp.exp(-m_row))
            b_last = b_col[L - 1:L, :]
            g_col = b_last - b_col + i_col
            m_new = jnp.maximum(b_last + m_prev, jnp.max(g_col, axis=0, keepdims=True))
            decay = jnp.exp(b_last + m_prev - m_new)
            ws = jnp.exp(g_col - m_new)
            kf = k.astype(F32)
            c_ref[h] = decay * cmat + _dot_tn(k, (v.astype(F32) * ws).astype(BF16))
            n_ref[h] = decay * nvec + jnp.sum(kf * ws, axis=0, keepdims=True)
            m_ref[h] = m_new
            mu = jnp.mean(hh, axis=1, keepdims=True)
            hc = hh - mu
            var = jnp.mean(hc * hc, axis=1, keepdims=True)
            hn = hc * lax.rsqrt(var + LN_EPS) * nw_ref[:, h * ML_DV:(h + 1) * ML_DV]
            gate = og_ref[rows, h * ML_DV:(h + 1) * ML_DV].astype(F32)
            out_ref[rows, h * ML_DV:(h + 1) * ML_DV] = (gate * hn).astype(BF16)
        return carry

    lax.fori_loop(0, seq // L, chunk, 0)


def _mlstm(qk, v, og, gif, ml_norm_w, batch, seq):
    t, d = v.shape
    seq_blk = lambda w: pl.BlockSpec((seq, w), lambda b: (b, 0))
    return pl.pallas_call(
        _mlstm_kernel,
        grid=(batch,),
        in_specs=[seq_blk(d), seq_blk(d), seq_blk(d), seq_blk(V7X_LANES), _whole_vmem()],
        out_specs=seq_blk(d),
        out_shape=jax.ShapeDtypeStruct((t, d), BF16),
        scratch_shapes=[pltpu.VMEM((ML_HEADS, ML_DQK, ML_DV), F32),
                        pltpu.VMEM((ML_HEADS, 1, ML_DQK), F32),
                        pltpu.VMEM((ML_HEADS, 1, 1), F32)],
        compiler_params=_params(1),
        name="mlstm",
    )(qk, v, og, gif, ml_norm_w)


def _diff_attn_kernel(lq1_ref, lk1_ref, lq2_ref, lk2_ref, q_ref, k_ref, vt_ref, nw_ref,
                      o_ref, q12_ref, st_ref, m_ref, acc_ref, *, lam_init):
    tq = q_ref.shape[0]
    tk = ATT_TILE
    heads = q_ref.shape[1] // DA_DV
    i = pl.program_id(2)
    lam = (jnp.exp(jnp.sum(lq1_ref[...] * lk1_ref[...], axis=1, keepdims=True))
           - jnp.exp(jnp.sum(lq2_ref[...] * lk2_ref[...], axis=1, keepdims=True)) + lam_init)

    lane = lax.broadcasted_iota(jnp.int32, (tq, DA_DV), 1)
    first = (lane & (DA_DHEAD // 2)) == 0
    for h in range(heads):
        q = q_ref[:, h * DA_DV:(h + 1) * DA_DV]
        zero = jnp.zeros_like(q)
        q12_ref[h, 0:tq, :] = jnp.where(first, q, zero)
        q12_ref[h, tq:2 * tq, :] = jnp.where(first, zero, q)

    vrows = DA_DV + ATT_ONES_ROWS

    def scores(j, h):
        rows = pl.ds(pl.multiple_of(j * tk, tk), tk)
        st_ref[h] = _dot_nt(k_ref[rows, h * DA_DV:(h + 1) * DA_DV], q12_ref[h])

    def softmax(h, mask_add):
        st = st_ref[h]
        if mask_add is not None:
            st = st + mask_add
        m_prev = m_ref[h]
        m_new = jnp.maximum(m_prev, jnp.max(st, axis=0, keepdims=True))
        alpha = jnp.exp2(m_prev - m_new)
        pt = jnp.exp2(st - m_new)
        m_ref[h] = m_new
        return alpha, pt.astype(BF16)

    def accumulate(j, h, alpha, pt):
        acc_ref[h] = alpha * acc_ref[h] + _dot(vt_ref[j, h * vrows:(h + 1) * vrows, :], pt)

    ahead = ATT_SCORE_LOOKAHEAD
    assert ahead <= heads
    for h in range(ahead):
        scores(0, h)
    m_ref[...] = jnp.full_like(m_ref, -jnp.inf)
    acc_ref[...] = jnp.zeros_like(acc_ref)

    def body(j, carry):
        for h in range(heads):
            alpha, pt = softmax(h, None)
            if h + ahead < heads:
                scores(j, h + ahead)
            else:
                scores(j + 1, h + ahead - heads)
            accumulate(j, h, alpha, pt)
        return carry

    lax.fori_loop(0, i, body, 0)
    kj = lax.broadcasted_iota(jnp.int32, (tk, 2 * tq), 0)
    qi = lax.broadcasted_iota(jnp.int32, (tk, 2 * tq), 1) & (tq - 1)
    causal_add = jnp.where(kj <= qi, 0.0, -jnp.inf)
    for h in range(heads):
        alpha, pt = softmax(h, causal_add)
        if h + ahead < heads:
            scores(i, h + ahead)
        accumulate(i, h, alpha, pt)

    for h in range(heads):
        acc = acc_ref[h, 0:DA_DV, :]
        r = 1.0 / acc_ref[h, DA_DV:DA_DV + 1, :]
        ot = acc[:, 0:tq] * r[:, 0:tq] - acc[:, tq:2 * tq] * (lam * r[:, tq:2 * tq])
        ms = jnp.mean(ot * ot, axis=0, keepdims=True)
        o = (ot * (lax.rsqrt(ms + LN_EPS) * (1.0 - lam_init))).T
        o_ref[:, h * DA_DV:(h + 1) * DA_DV] = (
            o * nw_ref[:, h * DA_DV:(h + 1) * DA_DV]).astype(BF16)


def _diff_attn(dq, dk, dvt, lam_vecs, da_norm_w, batch, seq, lam_init):
    t, d = dq.shape
    tq = ATT_TILE
    assert tq & (tq - 1) == 0
    nq = seq // tq
    hb = ATT_HEADS_PER_STEP
    w = hb * DA_DV
    vrows = DA_DV + ATT_ONES_ROWS
    lam_spec = pl.BlockSpec((1, DA_DHEAD), lambda b, g, i: (0, 0))
    q_spec = pl.BlockSpec((tq, w), lambda b, g, i: (b * nq + i, g))
    return pl.pallas_call(
        functools.partial(_diff_attn_kernel, lam_init=lam_init),
        grid=(batch, DA_HEADS // hb, nq),
        in_specs=[lam_spec] * 4 + [q_spec,
                                   pl.BlockSpec((seq, w), lambda b, g, i: (b, g)),
                                   pl.BlockSpec((nq, hb * vrows, tq), lambda b, g, i: (b, g, 0)),
                                   pl.BlockSpec((1, w), lambda b, g, i: (0, g))],
        out_specs=q_spec,
        out_shape=jax.ShapeDtypeStruct((t, d), BF16),
        scratch_shapes=[pltpu.VMEM((hb, 2 * tq, DA_DV), BF16),
                        pltpu.VMEM((hb, tq, 2 * tq), F32),
                        pltpu.VMEM((hb, 1, 2 * tq), F32),
                        pltpu.VMEM((hb, vrows, 2 * tq), F32)],
        compiler_params=_params(3),
        name="diff_attn",
    )(*lam_vecs, dq, dk, dvt, da_norm_w)


def _merge_kernel(x_ref, ha_ref, ob_ref, gate_ref, wa_ref, wb_ref, wm_ref, g_ref, b_ref, o_ref):
    d = x_ref.shape[1]
    ya = _dot(ha_ref[...], wa_ref[...])
    yb = _dot(ob_ref[...], wb_ref[...])
    mixin = gate_ref[:, 0:d].astype(F32) * ya + gate_ref[:, d:2 * d].astype(F32) * yb
    mix = _dot(mixin.astype(BF16), wm_ref[...])
    o_ref[...] = _layer_norm(ALPHA * x_ref[...] + mix, g_ref[...], b_ref[...])


def _merge(x2, ha, ob, gates, wa, wb, wm, g, b):
    t, d = x2.shape
    tm = ROW_TILE
    row_blk = lambda w: pl.BlockSpec((tm, w), lambda i: (i, 0))
    return pl.pallas_call(
        _merge_kernel,
        grid=(t // tm,),
        in_specs=[row_blk(d), row_blk(d), row_blk(d), row_blk(2 * d)] + [_whole_vmem()] * 5,
        out_specs=row_blk(d),
        out_shape=jax.ShapeDtypeStruct((t, d), F32),
        compiler_params=_params(1),
        name="merge_ln1",
    )(x2, ha, ob, gates, wa, wb, wm, g, b)


def _xattn_kv_kernel(mem_ref, wk_ref, wv_ref, k_ref, v_ref):
    mb = mem_ref[...].astype(BF16)
    k_ref[...] = _dot(mb, wk_ref[...]).astype(BF16)
    v_ref[...] = _dot(mb, wv_ref[...]).astype(BF16)


def _xattn_kv(mem2, wk, wv, mem_len):
    t, d = mem2.shape
    blk = pl.BlockSpec((mem_len, d), lambda b: (b, 0))
    return pl.pallas_call(
        _xattn_kv_kernel,
        grid=(t // mem_len,),
        in_specs=[blk, _whole_vmem(), _whole_vmem()],
        out_specs=[blk, blk],
        out_shape=[jax.ShapeDtypeStruct((t, d), BF16)] * 2,
        compiler_params=_params(1),
        name="xattn_kv",
    )(mem2, wk, wv)


def _xattn_kernel(x_ref, k_ref, v_ref, wq_ref, wo_ref, g_ref, b_ref, o_ref):
    d = x_ref.shape[1]
    dh = d // XA_HEADS
    x = x_ref[...]
    xq = (_dot(x.astype(BF16), wq_ref[...]) * (dh ** -0.5)).astype(BF16)
    heads = []
    for h in range(XA_HEADS):
        cols = slice(h * dh, (h + 1) * dh)
        s = _dot_nt(xq[:, cols], k_ref[:, cols])
        e = jnp.exp(s - jnp.max(s, axis=1, keepdims=True))
        p = e / jnp.sum(e, axis=1, keepdims=True)
        heads.append(_dot(p.astype(BF16), v_ref[:, cols]).astype(BF16))
    xo = _dot(jnp.concatenate(heads, axis=1), wo_ref[...])
    o_ref[...] = _layer_norm(ALPHA * x + xo, g_ref[...], b_ref[...])


def _xattn(x1, kx, vx, wq, wo, g, b, seq, mem_len):
    t, d = x1.shape
    tm = ROW_TILE
    tiles_per_seq = seq // tm
    row_blk = pl.BlockSpec((tm, d), lambda i: (i, 0))
    kv_blk = pl.BlockSpec((mem_len, d), lambda i: (i // tiles_per_seq, 0))
    return pl.pallas_call(
        _xattn_kernel,
        grid=(t // tm,),
        in_specs=[row_blk, kv_blk, kv_blk] + [_whole_vmem()] * 4,
        out_specs=row_blk,
        out_shape=jax.ShapeDtypeStruct((t, d), F32),
        compiler_params=_params(1),
        name="xattn_ln2",
    )(x1, kx, vx, wq, wo, g, b)


def _ffn_kernel(x_ref, wg_ref, wu_ref, wd_ref, g_ref, b_ref, o_ref):
    x = x_ref[...]
    xb = x.astype(BF16)
    gt = _dot(xb, wg_ref[...])
    up = _dot(xb, wu_ref[...])
    hid = (gt * jax.nn.sigmoid(gt) * up).astype(BF16)
    o_ref[...] = _layer_norm(ALPHA * x + _dot(hid, wd_ref[...]), g_ref[...], b_ref[...])


def _ffn(x2, wg, wu, wd, g, b):
    t, d = x2.shape
    tm = ROW_TILE
    row_blk = pl.BlockSpec((tm, d), lambda i: (i, 0))
    return pl.pallas_call(
        _ffn_kernel,
        grid=(t // tm,),
        in_specs=[row_blk] + [_whole_vmem()] * 5,
        out_specs=row_blk,
        out_shape=jax.ShapeDtypeStruct((t, d), F32),
        compiler_params=_params(1),
        name="ffn_ln3",
    )(x2, wg, wu, wd, g, b)


def _rope_lane_perm():
    half = DA_DHEAD // 2
    perm = []
    for h in range(DA_HEADS):
        base = h * DA_DV
        for part in range(2):
            for c in range(2):
                start = base + c * DA_DHEAD + part * half
                perm.extend(range(start, start + half))
    return np.asarray(perm, dtype=np.int32)


def kernel(x, mem, positions, w_in, b_in, conv_w, conv_b, ml_norm_w, lam_q1, lam_k1, lam_q2, lam_k2, da_norm_w, w_proj_a, w_proj_b, w_mix_out, ln1_g, ln1_b, w_xq, w_xk, w_xv, w_xo, ln2_g, ln2_b, w_ffn_gate, w_ffn_up, w_ffn_down, ln3_g, ln3_b):
    batch, seq, d = x.shape
    mem_len = mem.shape[1]
    depth = w_in.shape[0]
    assert depth == DEPTH and seq % ROW_TILE == 0 and seq % ML_CHUNK == 0 and seq % ATT_TILE == 0
    t = batch * seq

    ml_qk_w = 2 * ML_HEADS * ML_DQK
    ml_v_w = ML_HEADS * ML_DV
    da_w = DA_HEADS * DA_DV
    sizes = (ml_qk_w, ml_v_w, ml_v_w, 2 * ML_HEADS, da_w, da_w, da_w, 2 * d)
    offs = np.concatenate([[0], np.cumsum(sizes)])
    assert ml_qk_w == d and ml_v_w == d and da_w == d and offs[-1] == w_in.shape[2]
    perm = _rope_lane_perm()
    half = DA_DHEAD // 2
    invf = ROPE_THETA ** (-jnp.arange(half, dtype=F32) / half)
    invf = jnp.tile(invf, V7X_LANES // half).reshape(1, V7X_LANES)

    x2 = x.reshape(t, d)
    pos2 = positions.reshape(t, 1)
    mem2 = mem.reshape(batch * mem_len, d)

    for l in range(depth):
        def piece(a, j):
            return a[..., offs[j]:offs[j + 1]]

        def pack(a):
            pad = jnp.zeros(a.shape[:-1] + (V7X_LANES - 2 * ML_HEADS,), a.dtype)
            return jnp.concatenate(
                [piece(a, 0), piece(a, 1), piece(a, 2), piece(a, 4)[..., perm],
                 piece(a, 5)[..., perm], piece(a, 7), piece(a, 3), pad], axis=-1)

        w_packed = pack(w_in[l]).astype(BF16)
        b_packed = pack(b_in[l].reshape(1, -1))
        assert w_packed.shape[1] == _W_PACKED
        w_dvt = jnp.pad(piece(w_in[l], 6).T.reshape(DA_HEADS, DA_DV, d),
                        ((0, 0), (0, ATT_ONES_ROWS), (0, 0))).reshape(-1, d).astype(BF16)
        b_dvt = jnp.pad(piece(b_in[l], 6).reshape(DA_HEADS, DA_DV), ((0, 0), (0, ATT_ONES_ROWS)),
                        constant_values=1.0).reshape(-1, 1)

        qk, mv, og, dq, dk, dvt, gates, gif = _in_proj(
            x2, pos2, w_packed, b_packed, conv_w[l], conv_b[l].reshape(1, -1), invf,
            w_dvt, b_dvt, seq)

        ha = _mlstm(qk, mv, og, gif, ml_norm_w[l].reshape(1, -1), batch, seq)

        lam_init = 0.8 - 0.6 * math.exp(-0.3 * l)
        lam_vecs = [a[l].reshape(1, -1) for a in (lam_q1, lam_k1, lam_q2, lam_k2)]
        ob = _diff_attn(dq, dk, dvt, lam_vecs, da_norm_w[l].reshape(1, -1), batch, seq, lam_init)

        x2 = _merge(x2, ha, ob, gates, w_proj_a[l].astype(BF16), w_proj_b[l].astype(BF16),
                    w_mix_out[l].astype(BF16), ln1_g[l].reshape(1, -1), ln1_b[l].reshape(1, -1))

        kx, vx = _xattn_kv(mem2, w_xk[l].astype(BF16), w_xv[l].astype(BF16), mem_len)
        x2 = _xattn(x2, kx, vx, w_xq[l].astype(BF16), w_xo[l].astype(BF16),
                    ln2_g[l].reshape(1, -1), ln2_b[l].reshape(1, -1), seq, mem_len)

        x2 = _ffn(x2, w_ffn_gate[l].astype(BF16), w_ffn_up[l].astype(BF16),
                  w_ffn_down[l].astype(BF16), ln3_g[l].reshape(1, -1), ln3_b[l].reshape(1, -1))

    return x2.reshape(batch, seq, d)
```

```python
import functools
import math

import numpy as np
import jax
import jax.numpy as jnp
from jax import lax
from jax.experimental import pallas as pl
from jax.experimental.pallas import tpu as pltpu

ML_HEADS = 4
ML_DQK = 128
ML_DV = 256
ML_CONV = 4
DA_HEADS = 8
DA_DHEAD = 64
DA_DV = 2 * DA_DHEAD
XA_HEADS = 4
ROPE_THETA = 10000.0
DEPTH = 1
ALPHA = (2.0 * DEPTH) ** 0.25
LN_EPS = 1e-5

V7X_LANES = 128
V7X_SUBLANES = 8
V7X_VMEM_LIMIT = 56 * 1024 * 1024

ROW_TILE = 512
SLAB = 256
ML_CHUNK = SLAB
ML_ONES_ROWS = 16
ATT_TILE = SLAB
ATT_HEADS_PER_STEP = 8
ATT_SCORE_LOOKAHEAD = 2
ATT_ONES_ROWS = 16
LOG2_E = math.log2(math.e)

F32 = jnp.float32
BF16 = jnp.bfloat16


def _dot(a, b):
    return jnp.dot(a, b, preferred_element_type=F32)


def _dot_nt(a, b):
    return lax.dot_general(a, b, (((1,), (1,)), ((), ())), preferred_element_type=F32)


def _dot_tn(a, b):
    return lax.dot_general(a, b, (((0,), (0,)), ((), ())), preferred_element_type=F32)


def _layer_norm(z, g, b):
    mu = jnp.mean(z, axis=-1, keepdims=True)
    zc = z - mu
    var = jnp.mean(zc * zc, axis=-1, keepdims=True)
    return zc * lax.rsqrt(var + LN_EPS) * g + b


def _whole_vmem():
    return pl.BlockSpec(memory_space=pltpu.VMEM)


def _params(n_axes):
    return pltpu.CompilerParams(dimension_semantics=("arbitrary",) * n_axes,
                                vmem_limit_bytes=V7X_VMEM_LIMIT)


_C_QK, _C_DQ, _C_DK, _C_GATE, _C_IF = (0, 1024, 2048, 3072, 5120)
_W_PACKED = _C_IF + V7X_LANES
_MVT_ROWS = ML_HEADS * (ML_DV + ML_ONES_ROWS)
_DVT_ROWS = DA_HEADS * (DA_DV + ATT_ONES_ROWS)
_GT_ROWS = 16
_R_MV, _R_OG, _R_DV, _R_GT = 0, _MVT_ROWS, _MVT_ROWS + 1024, _MVT_ROWS + 1024 + _DVT_ROWS
_WT_ROWS = _R_GT + _GT_ROWS


def _in_proj_kernel(x_ref, pos_ref, w_ref, b_ref, cw_ref, cb_ref, invf_ref, wt_ref, bt_ref,
                    qk_ref, dq_ref, dk_ref, gate_ref, gif_ref, mvt_ref, ogt_ref, dvt_ref, gt_ref,
                    carry_ref, *, tiles_per_seq):
    tm = x_ref.shape[0]
    d = qk_ref.shape[1]
    xb = x_ref[...].astype(BF16)

    def proj(c0, width):
        return _dot(xb, w_ref[:, c0:c0 + width]) + b_ref[:, c0:c0 + width]

    @pl.when(pl.program_id(0) % tiles_per_seq == 0)
    def _():
        carry_ref[...] = jnp.zeros_like(carry_ref)

    p = proj(_C_QK, d)
    carry = carry_ref[...]
    row = lax.broadcasted_iota(jnp.int32, (V7X_SUBLANES, d), 0)
    acc = p * cw_ref[ML_CONV - 1:ML_CONV, :] + cb_ref[...]
    for k in range(1, ML_CONV):
        sh = pltpu.roll(p, k, 0)
        top = jnp.where(row < k, pltpu.roll(carry, k, 0), sh[0:V7X_SUBLANES, :])
        sh = jnp.concatenate([top, sh[V7X_SUBLANES:, :]], axis=0)
        acc = acc + sh * cw_ref[ML_CONV - 1 - k:ML_CONV - k, :]
    carry_ref[...] = p[tm - V7X_SUBLANES:tm, :]
    y = acc * jax.nn.sigmoid(acc)
    half = d // 2
    qk_ref[:, 0:half] = (y[:, 0:half] * (ML_DQK ** -0.5)).astype(BF16)
    qk_ref[:, half:d] = y[:, half:d].astype(BF16)

    ang = pos_ref[...].astype(F32) * invf_ref[...]
    cosv = jnp.cos(ang)
    sinv = jnp.sin(ang)
    lane = lax.broadcasted_iota(jnp.int32, (1, V7X_LANES), 1)
    ssin = jnp.where(lane < V7X_LANES // 2, -sinv, sinv)

    def rope_store(out_ref, c0, scale):
        pr = proj(c0, d)
        for h in range(d // V7X_LANES):
            xh = pr[:, h * V7X_LANES:(h + 1) * V7X_LANES]
            r = xh * cosv + pltpu.roll(xh, V7X_LANES // 2, 1) * ssin
            out_ref[:, h * V7X_LANES:(h + 1) * V7X_LANES] = (r * scale).astype(BF16)

    rope_store(dq_ref, _C_DQ, DA_DHEAD ** -0.5 * LOG2_E)
    rope_store(dk_ref, _C_DK, 1.0)
    for c in range(tm // SLAB):
        xc = xb[c * SLAB:(c + 1) * SLAB, :]
        rt = _dot_nt(wt_ref[...], xc) + bt_ref[...]
        mvt_ref[c] = rt[_R_MV:_R_OG, :].astype(BF16)
        ogt_ref[c] = jax.nn.sigmoid(rt[_R_OG:_R_DV, :]).astype(BF16)
        dvt_ref[c] = rt[_R_DV:_R_GT, :].astype(BF16)
        gt_ref[c] = rt[_R_GT:_WT_ROWS, :]

    gate_ref[:, 0:d] = jax.nn.sigmoid(proj(_C_GATE, d)).astype(BF16)
    gate_ref[:, d:2 * d] = jax.nn.sigmoid(proj(_C_GATE + d, d)).astype(BF16)
    gif_ref[...] = proj(_C_IF, V7X_LANES)


def _in_proj(x2, pos2, w_packed, b_packed, conv_w, conv_b, invf, w_t, b_t, seq):
    t, d = x2.shape
    tm = ROW_TILE
    row_blk = lambda w: pl.BlockSpec((tm, w), lambda i: (i, 0))
    slab_blk = lambda r: pl.BlockSpec((tm // SLAB, r, SLAB), lambda i: (i, 0, 0))
    row_out = lambda w, dt: jax.ShapeDtypeStruct((t, w), dt)
    slab_out = lambda r, dt: jax.ShapeDtypeStruct((t // SLAB, r, SLAB), dt)
    return pl.pallas_call(
        functools.partial(_in_proj_kernel, tiles_per_seq=seq // tm),
        grid=(t // tm,),
        in_specs=[row_blk(d), row_blk(1)] + [_whole_vmem()] * 7,
        out_specs=[row_blk(d), row_blk(d), row_blk(d), row_blk(2 * d), row_blk(V7X_LANES),
                   slab_blk(_MVT_ROWS), slab_blk(d), slab_blk(_DVT_ROWS), slab_blk(_GT_ROWS)],
        out_shape=[row_out(d, BF16), row_out(d, BF16), row_out(d, BF16), row_out(2 * d, BF16),
                   row_out(V7X_LANES, F32), slab_out(_MVT_ROWS, BF16), slab_out(d, BF16),
                   slab_out(_DVT_ROWS, BF16), slab_out(_GT_ROWS, F32)],
        scratch_shapes=[pltpu.VMEM((V7X_SUBLANES, d), F32)],
        compiler_params=_params(1),
        name="in_proj",
    )(x2, pos2, w_packed, b_packed, conv_w, conv_b, invf, w_t, b_t)


def _split3_bf16(a):
    hi = a.astype(BF16)
    r1 = a - hi.astype(F32)
    mid = r1.astype(BF16)
    lo = (r1 - mid.astype(F32)).astype(BF16)
    return hi, mid, lo


def _mlstm_kernel(qk_ref, vt_ref, ogt_ref, gif_ref, gt_ref, nw_ref, out_ref,
                  c_ref, m_ref, qkt_ref, cq_ref):
    seq = qk_ref.shape[0]
    L = ML_CHUNK
    vrows = ML_DV + ML_ONES_ROWS
    c_ref[...] = jnp.zeros_like(c_ref)
    m_ref[...] = jnp.zeros_like(m_ref)

    si = lax.broadcasted_iota(jnp.int32, (L, L), 0)
    ti = lax.broadcasted_iota(jnp.int32, (L, L), 1)
    tri_lo = jnp.where(ti <= si, 1.0, 0.0).astype(BF16)
    tri_up = jnp.where(si <= ti, 1.0, 0.0).astype(BF16)
    causal_t = si <= ti
    nw = jnp.concatenate([nw_ref[...]] * (L // V7X_LANES), axis=1)

    def chunk(c, carry):
        rows = pl.ds(pl.multiple_of(c * L, L), L)
        for h in range(ML_HEADS):
            q = qk_ref[rows, h * ML_DQK:(h + 1) * ML_DQK]
            k = qk_ref[rows, (ML_HEADS + h) * ML_DQK:(ML_HEADS + h + 1) * ML_DQK]
            qkt_ref[h] = _dot_nt(k, q)
            cq_ref[h] = _dot_nt(c_ref[h].astype(BF16), q)
        gt = gt_ref[c]
        r3 = _split3_bf16(jax.nn.log_sigmoid(gt))
        bc_row = _dot(r3[0], tri_up) + _dot(r3[1], tri_up) + _dot(r3[2], tri_up)
        g = gif_ref[rows, :]
        c3 = _split3_bf16(jax.nn.log_sigmoid(g))
        bc_col = _dot(tri_lo, c3[0]) + _dot(tri_lo, c3[1]) + _dot(tri_lo, c3[2])
        z = g - pltpu.roll(bc_col, V7X_LANES - ML_HEADS, 1)
        for h in range(ML_HEADS):
            b_row = bc_row[ML_HEADS + h:ML_HEADS + h + 1, :]
            i_row = gt[h:h + 1, :]
            r_col = z[:, h:h + 1]
            m_prev = m_ref[h]
            dmat = jnp.where(causal_t, b_row + r_col, -jnp.inf)
            inter = b_row + m_prev
            m_row = jnp.maximum(inter, jnp.max(dmat, axis=0, keepdims=True))
            w_inter = jnp.exp(inter - m_row)
            pt = (qkt_ref[h] * jnp.exp(dmat - m_row)).astype(BF16)
            vt = vt_ref[c, h * vrows:(h + 1) * vrows, :]
            num = w_inter * cq_ref[h] + _dot(vt, pt)
            den = num[ML_DV:ML_DV + 1, :]
            ht = num[0:ML_DV, :] * (1.0 / jnp.maximum(jnp.abs(den), jnp.exp(-m_row)))
            b_last = b_row[:, L - 1:L]
            g_row = b_last - b_row + i_row
            m_new = jnp.maximum(b_last + m_prev, jnp.max(g_row, axis=1, keepdims=True))
            decay = jnp.exp(b_last + m_prev - m_new)
            ws = jnp.exp(g_row - m_new)
            k = qk_ref[rows, (ML_HEADS + h) * ML_DQK:(ML_HEADS + h + 1) * ML_DQK]
            c_ref[h] = decay * c_ref[h] + _dot((vt.astype(F32) * ws).astype(BF16), k)
            m_ref[h] = m_new
            mu = jnp.mean(ht, axis=0, keepdims=True)
            hc = ht - mu
            var = jnp.mean(hc * hc, axis=0, keepdims=True)
            hn = hc * lax.rsqrt(var + LN_EPS) * nw[h * ML_DV:(h + 1) * ML_DV, :]
            gate = ogt_ref[c, h * ML_DV:(h + 1) * ML_DV, :].astype(F32)
            out_ref[rows, h * ML_DV:(h + 1) * ML_DV] = (gate * hn).T.astype(BF16)
        return carry

    lax.fori_loop(0, seq // L, chunk, 0)


def _mlstm(qk, mvt, ogt, gif, gt, nw_lanes, batch, seq):
    t, d = qk.shape
    L = ML_CHUNK
    nc = seq // L
    vrows = ML_DV + ML_ONES_ROWS
    seq_blk = lambda w: pl.BlockSpec((seq, w), lambda b: (b, 0))
    slab_blk = lambda r: pl.BlockSpec((nc, r, L), lambda b: (b, 0, 0))
    return pl.pallas_call(
        _mlstm_kernel,
        grid=(batch,),
        in_specs=[seq_blk(d), slab_blk(_MVT_ROWS), slab_blk(d), seq_blk(V7X_LANES),
                  slab_blk(_GT_ROWS), _whole_vmem()],
        out_specs=seq_blk(d),
        out_shape=jax.ShapeDtypeStruct((t, d), BF16),
        scratch_shapes=[pltpu.VMEM((ML_HEADS, vrows, ML_DQK), F32),
                        pltpu.VMEM((ML_HEADS, 1, 1), F32),
                        pltpu.VMEM((ML_HEADS, L, L), F32),
                        pltpu.VMEM((ML_HEADS, vrows, L), F32)],
        compiler_params=_params(1),
        name="mlstm",
    )(qk, mvt, ogt, gif, gt, nw_lanes)


def _diff_attn_kernel(lq1_ref, lk1_ref, lq2_ref, lk2_ref, q_ref, k_ref, vt_ref, nw_ref,
                      o_ref, q12_ref, st_ref, m_ref, acc_ref, *, lam_init):
    tq = q_ref.shape[0]
    tk = ATT_TILE
    heads = q_ref.shape[1] // DA_DV
    i = pl.program_id(2)
    lam = (jnp.exp(jnp.sum(lq1_ref[...] * lk1_ref[...], axis=1, keepdims=True))
           - jnp.exp(jnp.sum(lq2_ref[...] * lk2_ref[...], axis=1, keepdims=True)) + lam_init)

    lane = lax.broadcasted_iota(jnp.int32, (tq, DA_DV), 1)
    first = (lane & (DA_DHEAD // 2)) == 0
    for h in range(heads):
        q = q_ref[:, h * DA_DV:(h + 1) * DA_DV]
        zero = jnp.zeros_like(q)
        q12_ref[h, 0:tq, :] = jnp.where(first, q, zero)
        q12_ref[h, tq:2 * tq, :] = jnp.where(first, zero, q)

    vrows = DA_DV + ATT_ONES_ROWS

    def scores(j, h):
        rows = pl.ds(pl.multiple_of(j * tk, tk), tk)
        st_ref[h] = _dot_nt(k_ref[rows, h * DA_DV:(h + 1) * DA_DV], q12_ref[h])

    def softmax(h, mask_add):
        st = st_ref[h]
        if mask_add is not None:
            st = st + mask_add
        m_prev = m_ref[h]
        m_new = jnp.maximum(m_prev, jnp.max(st, axis=0, keepdims=True))
        alpha = jnp.exp2(m_prev - m_new)
        pt = jnp.exp2(st - m_new)
        m_ref[h] = m_new
        return alpha, pt.astype(BF16)

    def accumulate(j, h, alpha, pt):
        acc_ref[h] = alpha * acc_ref[h] + _dot(vt_ref[j, h * vrows:(h + 1) * vrows, :], pt)

    ahead = ATT_SCORE_LOOKAHEAD
    assert ahead <= heads
    for h in range(ahead):
        scores(0, h)
    m_ref[...] = jnp.full_like(m_ref, -jnp.inf)
    acc_ref[...] = jnp.zeros_like(acc_ref)

    def body(j, carry):
        for h in range(heads):
            alpha, pt = softmax(h, None)
            if h + ahead < heads:
                scores(j, h + ahead)
            else:
                scores(j + 1, h + ahead - heads)
            accumulate(j, h, alpha, pt)
        return carry

    lax.fori_loop(0, i, body, 0)
    kj = lax.broadcasted_iota(jnp.int32, (tk, 2 * tq), 0)
    qi = lax.broadcasted_iota(jnp.int32, (tk, 2 * tq), 1) & (tq - 1)
    causal_add = jnp.where(kj <= qi, 0.0, -jnp.inf)
    for h in range(heads):
        alpha, pt = softmax(h, causal_add)
        if h + ahead < heads:
            scores(i, h + ahead)
        accumulate(i, h, alpha, pt)

    for h in range(heads):
        acc = acc_ref[h, 0:DA_DV, :]
        r = 1.0 / acc_ref[h, DA_DV:DA_DV + 1, :]
        ot = acc[:, 0:tq] * r[:, 0:tq] - acc[:, tq:2 * tq] * (lam * r[:, tq:2 * tq])
        ms = jnp.mean(ot * ot, axis=0, keepdims=True)
        o = (ot * (lax.rsqrt(ms + LN_EPS) * (1.0 - lam_init))).T
        o_ref[:, h * DA_DV:(h + 1) * DA_DV] = (
            o * nw_ref[:, h * DA_DV:(h + 1) * DA_DV]).astype(BF16)


def _diff_attn(dq, dk, dvt, lam_vecs, da_norm_w, batch, seq, lam_init):
    t, d = dq.shape
    tq = ATT_TILE
    assert tq & (tq - 1) == 0
    nq = seq // tq
    hb = ATT_HEADS_PER_STEP
    w = hb * DA_DV
    vrows = DA_DV + ATT_ONES_ROWS
    lam_spec = pl.BlockSpec((1, DA_DHEAD), lambda b, g, i: (0, 0))
    q_spec = pl.BlockSpec((tq, w), lambda b, g, i: (b * nq + i, g))
    return pl.pallas_call(
        functools.partial(_diff_attn_kernel, lam_init=lam_init),
        grid=(batch, DA_HEADS // hb, nq),
        in_specs=[lam_spec] * 4 + [q_spec,
                                   pl.BlockSpec((seq, w), lambda b, g, i: (b, g)),
                                   pl.BlockSpec((nq, hb * vrows, tq), lambda b, g, i: (b, g, 0)),
                                   pl.BlockSpec((1, w), lambda b, g, i: (0, g))],
        out_specs=q_spec,
        out_shape=jax.ShapeDtypeStruct((t, d), BF16),
        scratch_shapes=[pltpu.VMEM((hb, 2 * tq, DA_DV), BF16),
                        pltpu.VMEM((hb, tq, 2 * tq), F32),
                        pltpu.VMEM((hb, 1, 2 * tq), F32),
                        pltpu.VMEM((hb, vrows, 2 * tq), F32)],
        compiler_params=_params(3),
        name="diff_attn",
    )(*lam_vecs, dq, dk, dvt, da_norm_w)


def _merge_kernel(x_ref, ha_ref, ob_ref, gate_ref, wa_ref, wb_ref, wm_ref, g_ref, b_ref, o_ref):
    d = x_ref.shape[1]
    ya = _dot(ha_ref[...], wa_ref[...])
    yb = _dot(ob_ref[...], wb_ref[...])
    mixin = gate_ref[:, 0:d].astype(F32) * ya + gate_ref[:, d:2 * d].astype(F32) * yb
    mix = _dot(mixin.astype(BF16), wm_ref[...])
    o_ref[...] = _layer_norm(ALPHA * x_ref[...] + mix, g_ref[...], b_ref[...])


def _merge(x2, ha, ob, gates, wa, wb, wm, g, b):
    t, d = x2.shape
    tm = ROW_TILE
    row_blk = lambda w: pl.BlockSpec((tm, w), lambda i: (i, 0))
    return pl.pallas_call(
        _merge_kernel,
        grid=(t // tm,),
        in_specs=[row_blk(d), row_blk(d), row_blk(d), row_blk(2 * d)] + [_whole_vmem()] * 5,
        out_specs=row_blk(d),
        out_shape=jax.ShapeDtypeStruct((t, d), F32),
        compiler_params=_params(1),
        name="merge_ln1",
    )(x2, ha, ob, gates, wa, wb, wm, g, b)


def _xattn_kv_kernel(mem_ref, wk_ref, wv_ref, k_ref, v_ref):
    mb = mem_ref[...].astype(BF16)
    k_ref[...] = _dot(mb, wk_ref[...]).astype(BF16)
    v_ref[...] = _dot(mb, wv_ref[...]).astype(BF16)


def _xattn_kv(mem2, wk, wv, mem_len):
    t, d = mem2.shape
    blk = pl.BlockSpec((mem_len, d), lambda b: (b, 0))
    return pl.pallas_call(
        _xattn_kv_kernel,
        grid=(t // mem_len,),
        in_specs=[blk, _whole_vmem(), _whole_vmem()],
        out_specs=[blk, blk],
        out_shape=[jax.ShapeDtypeStruct((t, d), BF16)] * 2,
        compiler_params=_params(1),
        name="xattn_kv",
    )(mem2, wk, wv)


def _xattn_kernel(x_ref, k_ref, v_ref, wq_ref, wo_ref, g_ref, b_ref, o_ref):
    d = x_ref.shape[1]
    dh = d // XA_HEADS
    x = x_ref[...]
    xq = (_dot(x.astype(BF16), wq_ref[...]) * (dh ** -0.5)).astype(BF16)
    heads = []
    for h in range(XA_HEADS):
        cols = slice(h * dh, (h + 1) * dh)
        s = _dot_nt(xq[:, cols], k_ref[:, cols])
        e = jnp.exp(s - jnp.max(s, axis=1, keepdims=True))
        p = e / jnp.sum(e, axis=1, keepdims=True)
        heads.append(_dot(p.astype(BF16), v_ref[:, cols]).astype(BF16))
    xo = _dot(jnp.concatenate(heads, axis=1), wo_ref[...])
    o_ref[...] = _layer_norm(ALPHA * x + xo, g_ref[...], b_ref[...])


def _xattn(x1, kx, vx, wq, wo, g, b, seq, mem_len):
    t, d = x1.shape
    tm = ROW_TILE
    tiles_per_seq = seq // tm
    row_blk = pl.BlockSpec((tm, d), lambda i: (i, 0))
    kv_blk = pl.BlockSpec((mem_len, d), lambda i: (i // tiles_per_seq, 0))
    return pl.pallas_call(
        _xattn_kernel,
        grid=(t // tm,),
        in_specs=[row_blk, kv_blk, kv_blk] + [_whole_vmem()] * 4,
        out_specs=row_blk,
        out_shape=jax.ShapeDtypeStruct((t, d), F32),
        compiler_params=_params(1),
        name="xattn_ln2",
    )(x1, kx, vx, wq, wo, g, b)


def _ffn_kernel(x_ref, wg_ref, wu_ref, wd_ref, g_ref, b_ref, o_ref):
    x = x_ref[...]
    xb = x.astype(BF16)
    gt = _dot(xb, wg_ref[...])
    up = _dot(xb, wu_ref[...])
    hid = (gt * jax.nn.sigmoid(gt) * up).astype(BF16)
    o_ref[...] = _layer_norm(ALPHA * x + _dot(hid, wd_ref[...]), g_ref[...], b_ref[...])


def _ffn(x2, wg, wu, wd, g, b):
    t, d = x2.shape
    tm = ROW_TILE
    row_blk = pl.BlockSpec((tm, d), lambda i: (i, 0))
    return pl.pallas_call(
        _ffn_kernel,
        grid=(t // tm,),
        in_specs=[row_blk] + [_whole_vmem()] * 5,
        out_specs=row_blk,
        out_shape=jax.ShapeDtypeStruct((t, d), F32),
        compiler_params=_params(1),
        name="ffn_ln3",
    )(x2, wg, wu, wd, g, b)


def _rope_lane_perm():
    half = DA_DHEAD // 2
    perm = []
    for h in range(DA_HEADS):
        base = h * DA_DV
        for part in range(2):
            for c in range(2):
                start = base + c * DA_DHEAD + part * half
                perm.extend(range(start, start + half))
    return np.asarray(perm, dtype=np.int32)


def kernel(x, mem, positions, w_in, b_in, conv_w, conv_b, ml_norm_w, lam_q1, lam_k1, lam_q2, lam_k2, da_norm_w, w_proj_a, w_proj_b, w_mix_out, ln1_g, ln1_b, w_xq, w_xk, w_xv, w_xo, ln2_g, ln2_b, w_ffn_gate, w_ffn_up, w_ffn_down, ln3_g, ln3_b):
    batch, seq, d = x.shape
    mem_len = mem.shape[1]
    depth = w_in.shape[0]
    assert depth == DEPTH and seq % ROW_TILE == 0 and seq % ML_CHUNK == 0 and seq % ATT_TILE == 0
    t = batch * seq

    ml_qk_w = 2 * ML_HEADS * ML_DQK
    ml_v_w = ML_HEADS * ML_DV
    da_w = DA_HEADS * DA_DV
    sizes = (ml_qk_w, ml_v_w, ml_v_w, 2 * ML_HEADS, da_w, da_w, da_w, 2 * d)
    offs = np.concatenate([[0], np.cumsum(sizes)])
    assert ml_qk_w == d and ml_v_w == d and da_w == d and offs[-1] == w_in.shape[2]
    perm = _rope_lane_perm()
    half = DA_DHEAD // 2
    invf = ROPE_THETA ** (-jnp.arange(half, dtype=F32) / half)
    invf = jnp.tile(invf, V7X_LANES // half).reshape(1, V7X_LANES)

    x2 = x.reshape(t, d)
    pos2 = positions.reshape(t, 1)
    mem2 = mem.reshape(batch * mem_len, d)

    for l in range(depth):
        def piece(a, j):
            return a[..., offs[j]:offs[j + 1]]

        def pack(a):
            pad = jnp.zeros(a.shape[:-1] + (V7X_LANES - 2 * ML_HEADS,), a.dtype)
            return jnp.concatenate(
                [piece(a, 0), piece(a, 4)[..., perm], piece(a, 5)[..., perm], piece(a, 7),
                 piece(a, 3), pad], axis=-1)

        w_packed = pack(w_in[l]).astype(BF16)
        b_packed = pack(b_in[l].reshape(1, -1))
        assert w_packed.shape[1] == _W_PACKED

        def with_ones(j, heads, dv, ones):
            wt = jnp.pad(piece(w_in[l], j).T.reshape(heads, dv, d), ((0, 0), (0, ones), (0, 0)))
            bt = jnp.pad(piece(b_in[l], j).reshape(heads, dv), ((0, 0), (0, ones)),
                         constant_values=1.0)
            return wt.reshape(-1, d), bt.reshape(-1)

        w_mvt, b_mvt = with_ones(1, ML_HEADS, ML_DV, ML_ONES_ROWS)
        w_dvt, b_dvt = with_ones(6, DA_HEADS, DA_DV, ATT_ONES_ROWS)
        gpad = _GT_ROWS - 2 * ML_HEADS
        w_t = jnp.concatenate([w_mvt, piece(w_in[l], 2).T, w_dvt,
                               jnp.pad(piece(w_in[l], 3).T, ((0, gpad), (0, 0)))], axis=0)
        b_t = jnp.concatenate([b_mvt, piece(b_in[l], 2), b_dvt,
                               jnp.pad(piece(b_in[l], 3), (0, gpad))]).reshape(-1, 1)
        assert w_t.shape[0] == _WT_ROWS

        qk, dq, dk, gates, gif, mvt, ogt, dvt, gt = _in_proj(
            x2, pos2, w_packed, b_packed, conv_w[l], conv_b[l].reshape(1, -1), invf,
            w_t.astype(BF16), b_t, seq)

        nw_lanes = jnp.broadcast_to(ml_norm_w[l].reshape(-1, 1), (ML_HEADS * ML_DV, V7X_LANES))
        ha = _mlstm(qk, mvt, ogt, gif, gt, nw_lanes, batch, seq)

        lam_init = 0.8 - 0.6 * math.exp(-0.3 * l)
        lam_vecs = [a[l].reshape(1, -1) for a in (lam_q1, lam_k1, lam_q2, lam_k2)]
        ob = _diff_attn(dq, dk, dvt, lam_vecs, da_norm_w[l].reshape(1, -1), batch, seq, lam_init)

        x2 = _merge(x2, ha, ob, gates, w_proj_a[l].astype(BF16), w_proj_b[l].astype(BF16),
                    w_mix_out[l].astype(BF16), ln1_g[l].reshape(1, -1), ln1_b[l].reshape(1, -1))

        kx, vx = _xattn_kv(mem2, w_xk[l].astype(BF16), w_xv[l].astype(BF16), mem_len)
        x2 = _xattn(x2, kx, vx, w_xq[l].astype(BF16), w_xo[l].astype(BF16),
                    ln2_g[l].reshape(1, -1), ln2_b[l].reshape(1, -1), seq, mem_len)

        x2 = _ffn(x2, w_ffn_gate[l].astype(BF16), w_ffn_up[l].astype(BF16),
                  w_ffn_down[l].astype(BF16), ln3_g[l].reshape(1, -1), ln3_b[l].reshape(1, -1))

    return x2.reshape(batch, seq, d)
```

```python
import functools
import math

import numpy as np
import jax
import jax.numpy as jnp
from jax import lax
from jax.experimental import pallas as pl
from jax.experimental.pallas import tpu as pltpu

ML_HEADS = 4
ML_DQK = 128
ML_DV = 256
ML_CONV = 4
DA_HEADS = 8
DA_DHEAD = 64
DA_DV = 2 * DA_DHEAD
XA_HEADS = 4
ROPE_THETA = 10000.0
DEPTH = 1
ALPHA = (2.0 * DEPTH) ** 0.25
LN_EPS = 1e-5

V7X_LANES = 128
V7X_SUBLANES = 8
V7X_VMEM_LIMIT = 56 * 1024 * 1024

ROW_TILE = 512
ROW_SUB = 256
SLAB = 256
ML_CHUNK = SLAB
ML_ONES_ROWS = 16
ATT_TILE = SLAB
ATT_HEADS_PER_STEP = 8
ATT_SCORE_LOOKAHEAD = 2
ATT_ONES_ROWS = 16
LOG2_E = math.log2(math.e)

F32 = jnp.float32
BF16 = jnp.bfloat16


def _dot(a, b):
    return jnp.dot(a, b, preferred_element_type=F32)


def _dot_nt(a, b):
    return lax.dot_general(a, b, (((1,), (1,)), ((), ())), preferred_element_type=F32)


def _dot_tn(a, b):
    return lax.dot_general(a, b, (((0,), (0,)), ((), ())), preferred_element_type=F32)


def _layer_norm(z, g, b):
    mu = jnp.mean(z, axis=-1, keepdims=True)
    zc = z - mu
    var = jnp.mean(zc * zc, axis=-1, keepdims=True)
    return zc * lax.rsqrt(var + LN_EPS) * g + b


def _whole_vmem():
    return pl.BlockSpec(memory_space=pltpu.VMEM)


def _params(n_axes):
    return pltpu.CompilerParams(dimension_semantics=("arbitrary",) * n_axes,
                                vmem_limit_bytes=V7X_VMEM_LIMIT)


_C_QK, _C_DQ, _C_DK, _C_GATE, _C_IF = (0, 1024, 2048, 3072, 5120)
_W_PACKED = _C_IF + V7X_LANES
_MVT_ROWS = ML_HEADS * (ML_DV + ML_ONES_ROWS)
_DVT_ROWS = DA_HEADS * (DA_DV + ATT_ONES_ROWS)
_GT_ROWS = 16
_R_MV, _R_OG, _R_DV, _R_GT = 0, _MVT_ROWS, _MVT_ROWS + 1024, _MVT_ROWS + 1024 + _DVT_ROWS
_WT_ROWS = _R_GT + _GT_ROWS


def _in_proj_kernel(x_ref, pos_ref, w_ref, b_ref, cw_ref, cb_ref, invf_ref, wt_ref, bt_ref,
                    qk_ref, dq_ref, dk_ref, gate_ref, gif_ref, mvt_ref, ogt_ref, dvt_ref, gt_ref,
                    pqk_ref, pdq_ref, pdk_ref, pg1_ref, pg2_ref, rt_ref, *, tiles_per_seq):
    tm = x_ref.shape[0]
    d = qk_ref.shape[1]
    half = d // 2

    halo = V7X_SUBLANES
    bqk = b_ref[:, _C_QK:_C_QK + d]

    @pl.when(pl.program_id(0) % tiles_per_seq == 0)
    def _():
        pqk_ref[0:halo, :] = jnp.broadcast_to(-bqk, (halo, d))

    taps = [cw_ref[j:j + 1, :] for j in range(ML_CONV)]
    conv_bias = cb_ref[...] + bqk * (taps[0] + taps[1] + taps[2] + taps[3])
    lane = lax.broadcasted_iota(jnp.int32, (1, V7X_LANES), 1)

    n_sub = tm // SLAB

    def proj_to(dst_ref, xb, c0):
        dst_ref[...] = _dot(xb, w_ref[:, c0:c0 + d])

    def bias(c0, width):
        return b_ref[:, c0:c0 + width]

    xb = x_ref[0:SLAB, :].astype(BF16)
    proj_to(pqk_ref.at[halo:halo + SLAB], xb, _C_QK)
    for c in range(n_sub):
        rs = slice(c * SLAB, (c + 1) * SLAB)
        proj_to(pdq_ref, xb, _C_DQ)
        proj_to(pdk_ref, xb, _C_DK)
        rt_ref[...] = _dot_nt(wt_ref[...], xb)

        acc = conv_bias
        for j in range(ML_CONV):
            back = ML_CONV - 1 - j
            acc = acc + pqk_ref[halo - back:halo - back + SLAB, :] * taps[j]
        pqk_ref[0:halo, :] = pqk_ref[SLAB:SLAB + halo, :]
        y = acc * jax.nn.sigmoid(acc)
        qk_ref[rs, 0:half] = (y[:, 0:half] * (ML_DQK ** -0.5)).astype(BF16)
        qk_ref[rs, half:d] = y[:, half:d].astype(BF16)

        proj_to(pg1_ref, xb, _C_GATE)
        proj_to(pg2_ref, xb, _C_GATE + d)
        gif_ref[rs, :] = _dot(xb, w_ref[:, _C_IF:_C_IF + V7X_LANES]) + bias(_C_IF, V7X_LANES)

        ang = pos_ref[rs, :].astype(F32) * invf_ref[...]
        cosv = jnp.cos(ang)
        sinv = jnp.sin(ang)
        ssin = jnp.where(lane < V7X_LANES // 2, -sinv, sinv)

        def rope_store(out_ref, p_ref, c0, scale):
            for h in range(d // V7X_LANES):
                cols = slice(h * V7X_LANES, (h + 1) * V7X_LANES)
                xh = p_ref[:, cols] + b_ref[:, c0 + h * V7X_LANES:c0 + (h + 1) * V7X_LANES]
                r = xh * cosv + pltpu.roll(xh, V7X_LANES // 2, 1) * ssin
                out_ref[rs, cols] = (r * scale).astype(BF16)

        rope_store(dq_ref, pdq_ref, _C_DQ, DA_DHEAD ** -0.5 * LOG2_E)
        rope_store(dk_ref, pdk_ref, _C_DK, 1.0)

        mvt_ref[c] = (rt_ref[_R_MV:_R_OG, :] + bt_ref[_R_MV:_R_OG, :]).astype(BF16)
        ogt_ref[c] = jax.nn.sigmoid(rt_ref[_R_OG:_R_DV, :] + bt_ref[_R_OG:_R_DV, :]).astype(BF16)
        dvt_ref[c] = (rt_ref[_R_DV:_R_GT, :] + bt_ref[_R_DV:_R_GT, :]).astype(BF16)
        gt_ref[c] = rt_ref[_R_GT:_WT_ROWS, :] + bt_ref[_R_GT:_WT_ROWS, :]

        if c + 1 < n_sub:
            xb = x_ref[(c + 1) * SLAB:(c + 2) * SLAB, :].astype(BF16)
            proj_to(pqk_ref.at[halo:halo + SLAB], xb, _C_QK)

        gate_ref[rs, 0:d] = jax.nn.sigmoid(pg1_ref[...] + bias(_C_GATE, d)).astype(BF16)
        gate_ref[rs, d:2 * d] = jax.nn.sigmoid(pg2_ref[...] + bias(_C_GATE + d, d)).astype(BF16)


def _in_proj(x2, pos2, w_packed, b_packed, conv_w, conv_b, invf, w_t, b_t, seq):
    t, d = x2.shape
    tm = ROW_TILE
    row_blk = lambda w: pl.BlockSpec((tm, w), lambda i: (i, 0))
    slab_blk = lambda r: pl.BlockSpec((tm // SLAB, r, SLAB), lambda i: (i, 0, 0))
    row_out = lambda w, dt: jax.ShapeDtypeStruct((t, w), dt)
    slab_out = lambda r, dt: jax.ShapeDtypeStruct((t // SLAB, r, SLAB), dt)
    return pl.pallas_call(
        functools.partial(_in_proj_kernel, tiles_per_seq=seq // tm),
        grid=(t // tm,),
        in_specs=[row_blk(d), row_blk(1)] + [_whole_vmem()] * 7,
        out_specs=[row_blk(d), row_blk(d), row_blk(d), row_blk(2 * d), row_blk(V7X_LANES),
                   slab_blk(_MVT_ROWS), slab_blk(d), slab_blk(_DVT_ROWS), slab_blk(_GT_ROWS)],
        out_shape=[row_out(d, BF16), row_out(d, BF16), row_out(d, BF16), row_out(2 * d, BF16),
                   row_out(V7X_LANES, F32), slab_out(_MVT_ROWS, BF16), slab_out(d, BF16),
                   slab_out(_DVT_ROWS, BF16), slab_out(_GT_ROWS, F32)],
        scratch_shapes=[pltpu.VMEM((V7X_SUBLANES + SLAB, d), F32)]
        + [pltpu.VMEM((SLAB, d), F32)] * 4 + [pltpu.VMEM((_WT_ROWS, SLAB), F32)],
        compiler_params=_params(1),
        name="in_proj",
    )(x2, pos2, w_packed, b_packed, conv_w, conv_b, invf, w_t, b_t)


def _split3_bf16(a):
    hi = a.astype(BF16)
    r1 = a - hi.astype(F32)
    mid = r1.astype(BF16)
    lo = (r1 - mid.astype(F32)).astype(BF16)
    return hi, mid, lo


def _mlstm_kernel(qk_ref, vt_ref, ogt_ref, gif_ref, gt_ref, nw_ref, out_ref,
                  c_ref, m_ref, qkt_ref, cq_ref):
    seq = qk_ref.shape[0]
    L = ML_CHUNK
    vrows = ML_DV + ML_ONES_ROWS
    c_ref[...] = jnp.zeros_like(c_ref)
    m_ref[...] = jnp.zeros_like(m_ref)

    si = lax.broadcasted_iota(jnp.int32, (L, L), 0)
    ti = lax.broadcasted_iota(jnp.int32, (L, L), 1)
    tri_lo = jnp.where(ti <= si, 1.0, 0.0).astype(BF16)
    tri_up = jnp.where(si <= ti, 1.0, 0.0).astype(BF16)
    causal_t = si <= ti
    nw = jnp.concatenate([nw_ref[...]] * (L // V7X_LANES), axis=1)

    def chunk(c, carry):
        rows = pl.ds(pl.multiple_of(c * L, L), L)
        for h in range(ML_HEADS):
            q = qk_ref[rows, h * ML_DQK:(h + 1) * ML_DQK]
            k = qk_ref[rows, (ML_HEADS + h) * ML_DQK:(ML_HEADS + h + 1) * ML_DQK]
            qkt_ref[h] = _dot_nt(k, q)
            cq_ref[h] = _dot_nt(c_ref[h].astype(BF16), q)
        gt = gt_ref[c]
        r3 = _split3_bf16(jax.nn.log_sigmoid(gt))
        bc_row = _dot(r3[0], tri_up) + _dot(r3[1], tri_up) + _dot(r3[2], tri_up)
        g = gif_ref[rows, :]
        c3 = _split3_bf16(jax.nn.log_sigmoid(g))
        bc_col = _dot(tri_lo, c3[0]) + _dot(tri_lo, c3[1]) + _dot(tri_lo, c3[2])
        z = g - pltpu.roll(bc_col, V7X_LANES - ML_HEADS, 1)
        for h in range(ML_HEADS):
            b_row = bc_row[ML_HEADS + h:ML_HEADS + h + 1, :]
            i_row = gt[h:h + 1, :]
            r_col = z[:, h:h + 1]
            m_prev = m_ref[h]
            dmat = jnp.where(causal_t, b_row + r_col, -jnp.inf)
            inter = b_row + m_prev
            m_row = jnp.maximum(inter, jnp.max(dmat, axis=0, keepdims=True))
            w_inter = jnp.exp(inter - m_row)
            pt = (qkt_ref[h] * jnp.exp(dmat - m_row)).astype(BF16)
            vt = vt_ref[c, h * vrows:(h + 1) * vrows, :]
            num = w_inter * cq_ref[h] + _dot(vt, pt)
            den = num[ML_DV:ML_DV + 1, :]
            ht = num[0:ML_DV, :] * (1.0 / jnp.maximum(jnp.abs(den), jnp.exp(-m_row)))
            b_last = b_row[:, L - 1:L]
            g_row = b_last - b_row + i_row
            m_new = jnp.maximum(b_last + m_prev, jnp.max(g_row, axis=1, keepdims=True))
            decay = jnp.exp(b_last + m_prev - m_new)
            ws = jnp.exp(g_row - m_new)
            k = qk_ref[rows, (ML_HEADS + h) * ML_DQK:(ML_HEADS + h + 1) * ML_DQK]
            c_ref[h] = decay * c_ref[h] + _dot((vt.astype(F32) * ws).astype(BF16), k)
            m_ref[h] = m_new
            mu = jnp.mean(ht, axis=0, keepdims=True)
            hc = ht - mu
            var = jnp.mean(hc * hc, axis=0, keepdims=True)
            hn = hc * lax.rsqrt(var + LN_EPS) * nw[h * ML_DV:(h + 1) * ML_DV, :]
            gate = ogt_ref[c, h * ML_DV:(h + 1) * ML_DV, :].astype(F32)
            out_ref[rows, h * ML_DV:(h + 1) * ML_DV] = (gate * hn).T.astype(BF16)
        return carry

    lax.fori_loop(0, seq // L, chunk, 0)


def _mlstm(qk, mvt, ogt, gif, gt, nw_lanes, batch, seq):
    t, d = qk.shape
    L = ML_CHUNK
    nc = seq // L
    vrows = ML_DV + ML_ONES_ROWS
    seq_blk = lambda w: pl.BlockSpec((seq, w), lambda b: (b, 0))
    slab_blk = lambda r: pl.BlockSpec((nc, r, L), lambda b: (b, 0, 0))
    return pl.pallas_call(
        _mlstm_kernel,
        grid=(batch,),
        in_specs=[seq_blk(d), slab_blk(_MVT_ROWS), slab_blk(d), seq_blk(V7X_LANES),
                  slab_blk(_GT_ROWS), _whole_vmem()],
        out_specs=seq_blk(d),
        out_shape=jax.ShapeDtypeStruct((t, d), BF16),
        scratch_shapes=[pltpu.VMEM((ML_HEADS, vrows, ML_DQK), F32),
                        pltpu.VMEM((ML_HEADS, 1, 1), F32),
                        pltpu.VMEM((ML_HEADS, L, L), F32),
                        pltpu.VMEM((ML_HEADS, vrows, L), F32)],
        compiler_params=_params(1),
        name="mlstm",
    )(qk, mvt, ogt, gif, gt, nw_lanes)


def _diff_attn_kernel(lq1_ref, lk1_ref, lq2_ref, lk2_ref, q_ref, k_ref, vt_ref, nw_ref,
                      o_ref, q12_ref, st_ref, m_ref, acc_ref, *, lam_init):
    tq = q_ref.shape[0]
    tk = ATT_TILE
    heads = q_ref.shape[1] // DA_DV
    i = pl.program_id(2)
    lam = (jnp.exp(jnp.sum(lq1_ref[...] * lk1_ref[...], axis=1, keepdims=True))
           - jnp.exp(jnp.sum(lq2_ref[...] * lk2_ref[...], axis=1, keepdims=True)) + lam_init)

    feat = lax.broadcasted_iota(jnp.int32, (DA_DV, tq), 0)
    first = (feat & (DA_DHEAD // 2)) == 0
    for h in range(heads):
        qt = q_ref[:, h * DA_DV:(h + 1) * DA_DV].astype(F32).T
        q12_ref[h, :, 0:tq] = jnp.where(first, qt, 0.0).astype(BF16)
        q12_ref[h, :, tq:2 * tq] = jnp.where(first, 0.0, qt).astype(BF16)

    vrows = DA_DV + ATT_ONES_ROWS

    def scores(j, h):
        rows = pl.ds(pl.multiple_of(j * tk, tk), tk)
        st_ref[h] = _dot(k_ref[rows, h * DA_DV:(h + 1) * DA_DV], q12_ref[h])

    def softmax(h, mask_add):
        st = st_ref[h]
        if mask_add is not None:
            st = st + mask_add
        m_prev = m_ref[h]
        m_new = jnp.maximum(m_prev, jnp.max(st, axis=0, keepdims=True))
        alpha = jnp.exp2(m_prev - m_new)
        pt = jnp.exp2(st - m_new)
        m_ref[h] = m_new
        return alpha, pt.astype(BF16)

    def accumulate(j, h, alpha, pt):
        acc_ref[h] = alpha * acc_ref[h] + _dot(vt_ref[j, h * vrows:(h + 1) * vrows, :], pt)

    ahead = ATT_SCORE_LOOKAHEAD
    assert ahead <= heads
    for h in range(ahead):
        scores(0, h)
    m_ref[...] = jnp.full_like(m_ref, -jnp.inf)
    acc_ref[...] = jnp.zeros_like(acc_ref)

    def body(j, carry):
        for h in range(heads):
            alpha, pt = softmax(h, None)
            if h + ahead < heads:
                scores(j, h + ahead)
            else:
                scores(j + 1, h + ahead - heads)
            accumulate(j, h, alpha, pt)
        return carry

    lax.fori_loop(0, i, body, 0)
    kj = lax.broadcasted_iota(jnp.int32, (tk, 2 * tq), 0)
    qi = lax.broadcasted_iota(jnp.int32, (tk, 2 * tq), 1) & (tq - 1)
    causal_add = jnp.where(kj <= qi, 0.0, -jnp.inf)
    for h in range(heads):
        alpha, pt = softmax(h, causal_add)
        if h + ahead < heads:
            scores(i, h + ahead)
        accumulate(i, h, alpha, pt)

    for h in range(heads):
        acc = acc_ref[h, 0:DA_DV, :]
        r = 1.0 / acc_ref[h, DA_DV:DA_DV + 1, :]
        ot = acc[:, 0:tq] * r[:, 0:tq] - acc[:, tq:2 * tq] * (lam * r[:, tq:2 * tq])
        ms = jnp.mean(ot * ot, axis=0, keepdims=True)
        o = (ot * (lax.rsqrt(ms + LN_EPS) * (1.0 - lam_init))).T
        o_ref[:, h * DA_DV:(h + 1) * DA_DV] = (
            o * nw_ref[:, h * DA_DV:(h + 1) * DA_DV]).astype(BF16)


def _diff_attn(dq, dk, dvt, lam_vecs, da_norm_w, batch, seq, lam_init):
    t, d = dq.shape
    tq = ATT_TILE
    assert tq & (tq - 1) == 0
    nq = seq // tq
    hb = ATT_HEADS_PER_STEP
    w = hb * DA_DV
    vrows = DA_DV + ATT_ONES_ROWS
    lam_spec = pl.BlockSpec((1, DA_DHEAD), lambda b, g, i: (0, 0))
    q_spec = pl.BlockSpec((tq, w), lambda b, g, i: (b * nq + i, g))
    return pl.pallas_call(
        functools.partial(_diff_attn_kernel, lam_init=lam_init),
        grid=(batch, DA_HEADS // hb, nq),
        in_specs=[lam_spec] * 4 + [q_spec,
                                   pl.BlockSpec((seq, w), lambda b, g, i: (b, g)),
                                   pl.BlockSpec((nq, hb * vrows, tq), lambda b, g, i: (b, g, 0)),
                                   pl.BlockSpec((1, w), lambda b, g, i: (0, g))],
        out_specs=q_spec,
        out_shape=jax.ShapeDtypeStruct((t, d), BF16),
        scratch_shapes=[pltpu.VMEM((hb, DA_DV, 2 * tq), BF16),
                        pltpu.VMEM((hb, tq, 2 * tq), F32),
                        pltpu.VMEM((hb, 1, 2 * tq), F32),
                        pltpu.VMEM((hb, vrows, 2 * tq), F32)],
        compiler_params=_params(3),
        name="diff_attn",
    )(*lam_vecs, dq, dk, dvt, da_norm_w)


def _row_subs(tm):
    return [slice(r * ROW_SUB, (r + 1) * ROW_SUB) for r in range(tm // ROW_SUB)]


def _merge_kernel(x_ref, ha_ref, ob_ref, gate_ref, wa_ref, wb_ref, wm_ref, g_ref, b_ref, o_ref):
    d = x_ref.shape[1]
    subs = _row_subs(x_ref.shape[0])
    yab = [(_dot(ha_ref[s, :], wa_ref[...]), _dot(ob_ref[s, :], wb_ref[...])) for s in subs]
    mix = []
    for s, (ya, yb) in zip(subs, yab):
        mixin = gate_ref[s, 0:d].astype(F32) * ya + gate_ref[s, d:2 * d].astype(F32) * yb
        mix.append(_dot(mixin.astype(BF16), wm_ref[...]))
    for s, mx in zip(subs, mix):
        o_ref[s, :] = _layer_norm(ALPHA * x_ref[s, :] + mx, g_ref[...], b_ref[...])


def _merge(x2, ha, ob, gates, wa, wb, wm, g, b):
    t, d = x2.shape
    tm = ROW_TILE
    row_blk = lambda w: pl.BlockSpec((tm, w), lambda i: (i, 0))
    return pl.pallas_call(
        _merge_kernel,
        grid=(t // tm,),
        in_specs=[row_blk(d), row_blk(d), row_blk(d), row_blk(2 * d)] + [_whole_vmem()] * 5,
        out_specs=row_blk(d),
        out_shape=jax.ShapeDtypeStruct((t, d), F32),
        compiler_params=_params(1),
        name="merge_ln1",
    )(x2, ha, ob, gates, wa, wb, wm, g, b)


def _xattn_kv_kernel(mem_ref, wk_ref, wv_ref, k_ref, v_ref):
    mb = mem_ref[...].astype(BF16)
    k_ref[...] = _dot(mb, wk_ref[...]).astype(BF16)
    v_ref[...] = _dot(mb, wv_ref[...]).astype(BF16)


def _xattn_kv(mem2, wk, wv, mem_len):
    t, d = mem2.shape
    blk = pl.BlockSpec((mem_len, d), lambda b: (b, 0))
    return pl.pallas_call(
        _xattn_kv_kernel,
        grid=(t // mem_len,),
        in_specs=[blk, _whole_vmem(), _whole_vmem()],
        out_specs=[blk, blk],
        out_shape=[jax.ShapeDtypeStruct((t, d), BF16)] * 2,
        compiler_params=_params(1),
        name="xattn_kv",
    )(mem2, wk, wv)


def _xattn_kernel(x_ref, k_ref, v_ref, wq_ref, wo_ref, g_ref, b_ref, o_ref):
    d = x_ref.shape[1]
    dh = d // XA_HEADS
    subs = _row_subs(x_ref.shape[0])
    xq = [(_dot(x_ref[s, :].astype(BF16), wq_ref[...]) * (dh ** -0.5)).astype(BF16) for s in subs]
    xo = []
    for q in xq:
        heads = []
        for h in range(XA_HEADS):
            cols = slice(h * dh, (h + 1) * dh)
            sc = _dot_nt(q[:, cols], k_ref[:, cols])
            e = jnp.exp(sc - jnp.max(sc, axis=1, keepdims=True))
            p = e / jnp.sum(e, axis=1, keepdims=True)
            heads.append(_dot(p.astype(BF16), v_ref[:, cols]).astype(BF16))
        xo.append(_dot(jnp.concatenate(heads, axis=1), wo_ref[...]))
    for s, o in zip(subs, xo):
        o_ref[s, :] = _layer_norm(ALPHA * x_ref[s, :] + o, g_ref[...], b_ref[...])


def _xattn(x1, kx, vx, wq, wo, g, b, seq, mem_len):
    t, d = x1.shape
    tm = ROW_TILE
    tiles_per_seq = seq // tm
    row_blk = pl.BlockSpec((tm, d), lambda i: (i, 0))
    kv_blk = pl.BlockSpec((mem_len, d), lambda i: (i // tiles_per_seq, 0))
    return pl.pallas_call(
        _xattn_kernel,
        grid=(t // tm,),
        in_specs=[row_blk, kv_blk, kv_blk] + [_whole_vmem()] * 4,
        out_specs=row_blk,
        out_shape=jax.ShapeDtypeStruct((t, d), F32),
        compiler_params=_params(1),
        name="xattn_ln2",
    )(x1, kx, vx, wq, wo, g, b)


def _ffn_kernel(x_ref, wg_ref, wu_ref, wd_ref, g_ref, b_ref, o_ref):
    subs = _row_subs(x_ref.shape[0])
    gu = []
    for s in subs:
        xb = x_ref[s, :].astype(BF16)
        gu.append((_dot(xb, wg_ref[...]), _dot(xb, wu_ref[...])))
    down = [_dot((gt * jax.nn.sigmoid(gt) * up).astype(BF16), wd_ref[...]) for gt, up in gu]
    for s, dn in zip(subs, down):
        o_ref[s, :] = _layer_norm(ALPHA * x_ref[s, :] + dn, g_ref[...], b_ref[...])


def _ffn(x2, wg, wu, wd, g, b):
    t, d = x2.shape
    tm = ROW_TILE
    row_blk = pl.BlockSpec((tm, d), lambda i: (i, 0))
    return pl.pallas_call(
        _ffn_kernel,
        grid=(t // tm,),
        in_specs=[row_blk] + [_whole_vmem()] * 5,
        out_specs=row_blk,
        out_shape=jax.ShapeDtypeStruct((t, d), F32),
        compiler_params=_params(1),
        name="ffn_ln3",
    )(x2, wg, wu, wd, g, b)


def _rope_lane_perm():
    half = DA_DHEAD // 2
    perm = []
    for h in range(DA_HEADS):
        base = h * DA_DV
        for part in range(2):
            for c in range(2):
                start = base + c * DA_DHEAD + part * half
                perm.extend(range(start, start + half))
    return np.asarray(perm, dtype=np.int32)


def kernel(x, mem, positions, w_in, b_in, conv_w, conv_b, ml_norm_w, lam_q1, lam_k1, lam_q2, lam_k2, da_norm_w, w_proj_a, w_proj_b, w_mix_out, ln1_g, ln1_b, w_xq, w_xk, w_xv, w_xo, ln2_g, ln2_b, w_ffn_gate, w_ffn_up, w_ffn_down, ln3_g, ln3_b):
    batch, seq, d = x.shape
    mem_len = mem.shape[1]
    depth = w_in.shape[0]
    assert depth == DEPTH and seq % ROW_TILE == 0 and seq % ML_CHUNK == 0 and seq % ATT_TILE == 0
    t = batch * seq

    ml_qk_w = 2 * ML_HEADS * ML_DQK
    ml_v_w = ML_HEADS * ML_DV
    da_w = DA_HEADS * DA_DV
    sizes = (ml_qk_w, ml_v_w, ml_v_w, 2 * ML_HEADS, da_w, da_w, da_w, 2 * d)
    offs = np.concatenate([[0], np.cumsum(sizes)])
    assert ml_qk_w == d and ml_v_w == d and da_w == d and offs[-1] == w_in.shape[2]
    perm = _rope_lane_perm()
    half = DA_DHEAD // 2
    invf = ROPE_THETA ** (-jnp.arange(half, dtype=F32) / half)
    invf = jnp.tile(invf, V7X_LANES // half).reshape(1, V7X_LANES)

    x2 = x.reshape(t, d)
    pos2 = positions.reshape(t, 1)
    mem2 = mem.reshape(batch * mem_len, d)

    for l in range(depth):
        def piece(a, j):
            return a[..., offs[j]:offs[j + 1]]

        def pack(a):
            pad = jnp.zeros(a.shape[:-1] + (V7X_LANES - 2 * ML_HEADS,), a.dtype)
            return jnp.concatenate(
                [piece(a, 0), piece(a, 4)[..., perm], piece(a, 5)[..., perm], piece(a, 7),
                 piece(a, 3), pad], axis=-1)

        w_packed = pack(w_in[l]).astype(BF16)
        b_packed = pack(b_in[l].reshape(1, -1))
        assert w_packed.shape[1] == _W_PACKED

        def with_ones(j, heads, dv, ones):
            wt = jnp.pad(piece(w_in[l], j).T.reshape(heads, dv, d), ((0, 0), (0, ones), (0, 0)))
            bt = jnp.pad(piece(b_in[l], j).reshape(heads, dv), ((0, 0), (0, ones)),
                         constant_values=1.0)
            return wt.reshape(-1, d), bt.reshape(-1)

        w_mvt, b_mvt = with_ones(1, ML_HEADS, ML_DV, ML_ONES_ROWS)
        w_dvt, b_dvt = with_ones(6, DA_HEADS, DA_DV, ATT_ONES_ROWS)
        gpad = _GT_ROWS - 2 * ML_HEADS
        w_t = jnp.concatenate([w_mvt, piece(w_in[l], 2).T, w_dvt,
                               jnp.pad(piece(w_in[l], 3).T, ((0, gpad), (0, 0)))], axis=0)
        b_t = jnp.concatenate([b_mvt, piece(b_in[l], 2), b_dvt,
                               jnp.pad(piece(b_in[l], 3), (0, gpad))]).reshape(-1, 1)
        assert w_t.shape[0] == _WT_ROWS

        qk, dq, dk, gates, gif, mvt, ogt, dvt, gt = _in_proj(
            x2, pos2, w_packed, b_packed, conv_w[l], conv_b[l].reshape(1, -1), invf,
            w_t.astype(BF16), b_t, seq)

        nw_lanes = jnp.broadcast_to(ml_norm_w[l].reshape(-1, 1), (ML_HEADS * ML_DV, V7X_LANES))
        ha = _mlstm(qk, mvt, ogt, gif, gt, nw_lanes, batch, seq)

        lam_init = 0.8 - 0.6 * math.exp(-0.3 * l)
        lam_vecs = [a[l].reshape(1, -1) for a in (lam_q1, lam_k1, lam_q2, lam_k2)]
        ob = _diff_attn(dq, dk, dvt, lam_vecs, da_norm_w[l].reshape(1, -1), batch, seq, lam_init)

        x2 = _merge(x2, ha, ob, gates, w_proj_a[l].astype(BF16), w_proj_b[l].astype(BF16),
                    w_mix_out[l].astype(BF16), ln1_g[l].reshape(1, -1), ln1_b[l].reshape(1, -1))

        kx, vx = _xattn_kv(mem2, w_xk[l].astype(BF16), w_xv[l].astype(BF16), mem_len)
        x2 = _xattn(x2, kx, vx, w_xq[l].astype(BF16), w_xo[l].astype(BF16),
                    ln2_g[l].reshape(1, -1), ln2_b[l].reshape(1, -1), seq, mem_len)

        x2 = _ffn(x2, w_ffn_gate[l].astype(BF16), w_ffn_up[l].astype(BF16),
                  w_ffn_down[l].astype(BF16), ln3_g[l].reshape(1, -1), ln3_b[l].reshape(1, -1))

    return x2.reshape(batch, seq, d)
```

```python
import functools
import math

import numpy as np
import jax
import jax.numpy as jnp
from jax import lax
from jax.experimental import pallas as pl
from jax.experimental.pallas import tpu as pltpu

ML_HEADS = 4
ML_DQK = 128
ML_DV = 256
ML_CONV = 4
DA_HEADS = 8
DA_DHEAD = 64
DA_DV = 2 * DA_DHEAD
XA_HEADS = 4
ROPE_THETA = 10000.0
DEPTH = 1
ALPHA = (2.0 * DEPTH) ** 0.25
LN_EPS = 1e-5

V7X_LANES = 128
V7X_SUBLANES = 8
V7X_VMEM_LIMIT = 56 * 1024 * 1024

ROW_TILE = 512
ROW_SUB = 256
SLAB = 256
ML_CHUNK = SLAB
ML_ONES_ROWS = 16
ATT_TILE = SLAB
ATT_HEADS_PER_STEP = 8
ATT_SCORE_LOOKAHEAD = 2
ATT_ONES_ROWS = 16
LOG2_E = math.log2(math.e)
ROPE_PACK = V7X_LANES // (DA_DHEAD // 2)

F32 = jnp.float32
BF16 = jnp.bfloat16


def _dot(a, b):
    return jnp.dot(a, b, preferred_element_type=F32)


def _dot_nt(a, b):
    return lax.dot_general(a, b, (((1,), (1,)), ((), ())), preferred_element_type=F32)


def _dot_tn(a, b):
    return lax.dot_general(a, b, (((0,), (0,)), ((), ())), preferred_element_type=F32)


def _layer_norm(z, g, b):
    mu = jnp.mean(z, axis=-1, keepdims=True)
    zc = z - mu
    var = jnp.mean(zc * zc, axis=-1, keepdims=True)
    return zc * lax.rsqrt(var + LN_EPS) * g + b


def _whole_vmem():
    return pl.BlockSpec(memory_space=pltpu.VMEM)


def _params(n_axes):
    return pltpu.CompilerParams(dimension_semantics=("arbitrary",) * n_axes,
                                vmem_limit_bytes=V7X_VMEM_LIMIT)


_C_QK, _C_DQ, _C_DK, _C_GATE, _C_IF = (0, 1024, 2048, 3072, 5120)
_W_PACKED = _C_IF + V7X_LANES
_MVT_ROWS = ML_HEADS * (ML_DV + ML_ONES_ROWS)
_DVT_ROWS = DA_HEADS * (DA_DV + ATT_ONES_ROWS)
_GT_ROWS = 16
_R_MV, _R_OG, _R_DV, _R_GT = 0, _MVT_ROWS, _MVT_ROWS + 1024, _MVT_ROWS + 1024 + _DVT_ROWS
_WT_ROWS = _R_GT + _GT_ROWS


def _in_proj_kernel(x_ref, pos_ref, w_ref, b_ref, cw_ref, cb_ref, invf_ref, wt_ref, bt_ref,
                    qk_ref, dq_ref, dk_ref, gate_ref, gif_ref, mvt_ref, ogt_ref, dvt_ref, gt_ref,
                    pqk_ref, pdq_ref, pdk_ref, pg1_ref, pg2_ref, rt_ref, *, tiles_per_seq):
    tm = x_ref.shape[0]
    d = qk_ref.shape[1]
    half = d // 2

    halo = V7X_SUBLANES
    bqk = b_ref[:, _C_QK:_C_QK + d]

    @pl.when(pl.program_id(0) % tiles_per_seq == 0)
    def _():
        pqk_ref[0:halo, :] = jnp.broadcast_to(-bqk, (halo, d))

    taps = [cw_ref[j:j + 1, :] for j in range(ML_CONV)]
    conv_bias = cb_ref[...] + bqk * (taps[0] + taps[1] + taps[2] + taps[3])
    lane = lax.broadcasted_iota(jnp.int32, (1, V7X_LANES), 1)
    rope_w = V7X_LANES // ROPE_PACK

    n_sub = tm // SLAB

    def proj_to(dst_ref, xb, c0):
        dst_ref[...] = _dot(xb, w_ref[:, c0:c0 + d])

    def bias(c0, width):
        return b_ref[:, c0:c0 + width]

    xb = x_ref[0:SLAB, :].astype(BF16)
    proj_to(pqk_ref.at[halo:halo + SLAB], xb, _C_QK)
    for c in range(n_sub):
        rs = slice(c * SLAB, (c + 1) * SLAB)
        proj_to(pdq_ref, xb, _C_DQ)
        proj_to(pdk_ref, xb, _C_DK)
        rt_ref[...] = _dot_nt(wt_ref[...], xb)

        acc = conv_bias
        for j in range(ML_CONV):
            back = ML_CONV - 1 - j
            acc = acc + pqk_ref[halo - back:halo - back + SLAB, :] * taps[j]
        pqk_ref[0:halo, :] = pqk_ref[SLAB:SLAB + halo, :]
        y = acc * jax.nn.sigmoid(acc)
        qk_ref[rs, 0:half] = (y[:, 0:half] * (ML_DQK ** -0.5)).astype(BF16)
        qk_ref[rs, half:d] = y[:, half:d].astype(BF16)

        proj_to(pg1_ref, xb, _C_GATE)
        proj_to(pg2_ref, xb, _C_GATE + d)
        gif_ref[rs, :] = _dot(xb, w_ref[:, _C_IF:_C_IF + V7X_LANES]) + bias(_C_IF, V7X_LANES)

        pk = SLAB // ROPE_PACK
        ang = pos_ref[c * pk:(c + 1) * pk, :].astype(F32) * invf_ref[...]
        cos_p = jnp.cos(ang)
        sin_p = jnp.sin(ang)

        def spread(v):
            out = []
            for g in range(ROPE_PACK):
                u = v if g == 0 else pltpu.roll(v, V7X_LANES - g * rope_w, 1)
                u = jnp.where(lane < rope_w, u, pltpu.roll(u, rope_w, 1))
                out.append(jnp.where(lane < 2 * rope_w, u, pltpu.roll(u, 2 * rope_w, 1)))
            return jnp.concatenate(out, axis=0)

        cosv = spread(cos_p)
        sinv = spread(sin_p)
        ssin = jnp.where(lane < V7X_LANES // 2, -sinv, sinv)

        def rope_store(out_ref, p_ref, c0, scale):
            for h in range(d // V7X_LANES):
                cols = slice(h * V7X_LANES, (h + 1) * V7X_LANES)
                xh = p_ref[:, cols] + b_ref[:, c0 + h * V7X_LANES:c0 + (h + 1) * V7X_LANES]
                r = xh * cosv + pltpu.roll(xh, V7X_LANES // 2, 1) * ssin
                out_ref[rs, cols] = (r * scale).astype(BF16)

        rope_store(dq_ref, pdq_ref, _C_DQ, DA_DHEAD ** -0.5 * LOG2_E)
        rope_store(dk_ref, pdk_ref, _C_DK, 1.0)

        mvt_ref[c] = (rt_ref[_R_MV:_R_OG, :] + bt_ref[_R_MV:_R_OG, :]).astype(BF16)
        ogt_ref[c] = jax.nn.sigmoid(rt_ref[_R_OG:_R_DV, :] + bt_ref[_R_OG:_R_DV, :]).astype(BF16)
        dvt_ref[c] = (rt_ref[_R_DV:_R_GT, :] + bt_ref[_R_DV:_R_GT, :]).astype(BF16)
        gt_ref[c] = rt_ref[_R_GT:_WT_ROWS, :] + bt_ref[_R_GT:_WT_ROWS, :]

        if c + 1 < n_sub:
            xb = x_ref[(c + 1) * SLAB:(c + 2) * SLAB, :].astype(BF16)
            proj_to(pqk_ref.at[halo:halo + SLAB], xb, _C_QK)

        gate_ref[rs, 0:d] = jax.nn.sigmoid(pg1_ref[...] + bias(_C_GATE, d)).astype(BF16)
        gate_ref[rs, d:2 * d] = jax.nn.sigmoid(pg2_ref[...] + bias(_C_GATE + d, d)).astype(BF16)


def _in_proj(x2, pos2, w_packed, b_packed, conv_w, conv_b, invf, w_t, b_t, seq):
    t, d = x2.shape
    tm = ROW_TILE
    row_blk = lambda w: pl.BlockSpec((tm, w), lambda i: (i, 0))
    slab_blk = lambda r: pl.BlockSpec((tm // SLAB, r, SLAB), lambda i: (i, 0, 0))
    row_out = lambda w, dt: jax.ShapeDtypeStruct((t, w), dt)
    slab_out = lambda r, dt: jax.ShapeDtypeStruct((t // SLAB, r, SLAB), dt)
    return pl.pallas_call(
        functools.partial(_in_proj_kernel, tiles_per_seq=seq // tm),
        grid=(t // tm,),
        in_specs=[row_blk(d), pl.BlockSpec((tm // ROPE_PACK, V7X_LANES), lambda i: (i, 0))]
        + [_whole_vmem()] * 7,
        out_specs=[row_blk(d), row_blk(d), row_blk(d), row_blk(2 * d), row_blk(V7X_LANES),
                   slab_blk(_MVT_ROWS), slab_blk(d), slab_blk(_DVT_ROWS), slab_blk(_GT_ROWS)],
        out_shape=[row_out(d, BF16), row_out(d, BF16), row_out(d, BF16), row_out(2 * d, BF16),
                   row_out(V7X_LANES, F32), slab_out(_MVT_ROWS, BF16), slab_out(d, BF16),
                   slab_out(_DVT_ROWS, BF16), slab_out(_GT_ROWS, F32)],
        scratch_shapes=[pltpu.VMEM((V7X_SUBLANES + SLAB, d), F32)]
        + [pltpu.VMEM((SLAB, d), F32)] * 4 + [pltpu.VMEM((_WT_ROWS, SLAB), F32)],
        compiler_params=_params(1),
        name="in_proj",
    )(x2, pos2, w_packed, b_packed, conv_w, conv_b, invf, w_t, b_t)


def _split3_bf16(a):
    hi = a.astype(BF16)
    r1 = a - hi.astype(F32)
    mid = r1.astype(BF16)
    lo = (r1 - mid.astype(F32)).astype(BF16)
    return hi, mid, lo


def _mlstm_kernel(qk_ref, vt_ref, ogt_ref, gif_ref, gt_ref, nw_ref, out_ref,
                  c_ref, m_ref, qkt_ref, cq_ref):
    seq = qk_ref.shape[0]
    L = ML_CHUNK
    vrows = ML_DV + ML_ONES_ROWS
    c_ref[...] = jnp.zeros_like(c_ref)
    m_ref[...] = jnp.zeros_like(m_ref)

    si = lax.broadcasted_iota(jnp.int32, (L, L), 0)
    ti = lax.broadcasted_iota(jnp.int32, (L, L), 1)
    tri_lo = jnp.where(ti <= si, 1.0, 0.0).astype(BF16)
    tri_up = jnp.where(si <= ti, 1.0, 0.0).astype(BF16)
    causal_t = si <= ti
    nw = jnp.concatenate([nw_ref[...]] * (L // V7X_LANES), axis=1)

    def chunk(c, carry):
        rows = pl.ds(pl.multiple_of(c * L, L), L)
        for h in range(ML_HEADS):
            q = qk_ref[rows, h * ML_DQK:(h + 1) * ML_DQK]
            k = qk_ref[rows, (ML_HEADS + h) * ML_DQK:(ML_HEADS + h + 1) * ML_DQK]
            qkt_ref[h] = _dot_nt(k, q)
            cq_ref[h] = _dot_nt(c_ref[h].astype(BF16), q)
        gt = gt_ref[c]
        r3 = _split3_bf16(jax.nn.log_sigmoid(gt))
        bc_row = _dot(r3[0], tri_up) + _dot(r3[1], tri_up) + _dot(r3[2], tri_up)
        g = gif_ref[rows, :]
        c3 = _split3_bf16(jax.nn.log_sigmoid(g))
        bc_col = _dot(tri_lo, c3[0]) + _dot(tri_lo, c3[1]) + _dot(tri_lo, c3[2])
        z = g - pltpu.roll(bc_col, V7X_LANES - ML_HEADS, 1)
        for h in range(ML_HEADS):
            b_row = bc_row[ML_HEADS + h:ML_HEADS + h + 1, :]
            i_row = gt[h:h + 1, :]
            r_col = z[:, h:h + 1]
            m_prev = m_ref[h]
            dmat = jnp.where(causal_t, b_row + r_col, -jnp.inf)
            inter = b_row + m_prev
            m_row = jnp.maximum(inter, jnp.max(dmat, axis=0, keepdims=True))
            w_inter = jnp.exp(inter - m_row)
            pt = (qkt_ref[h] * jnp.exp(dmat - m_row)).astype(BF16)
            vt = vt_ref[c, h * vrows:(h + 1) * vrows, :]
            num = w_inter * cq_ref[h] + _dot(vt, pt)
            den = num[ML_DV:ML_DV + 1, :]
            rden = 1.0 / jnp.maximum(jnp.abs(den), jnp.exp(-m_row))
            b_last = b_row[:, L - 1:L]
            g_row = b_last - b_row + i_row
            m_new = jnp.maximum(b_last + m_prev, jnp.max(g_row, axis=1, keepdims=True))
            decay = jnp.exp(b_last + m_prev - m_new)
            ws = jnp.exp(g_row - m_new)
            k = qk_ref[rows, (ML_HEADS + h) * ML_DQK:(ML_HEADS + h + 1) * ML_DQK]
            c_ref[h] = decay * c_ref[h] + _dot((vt.astype(F32) * ws).astype(BF16), k)
            m_ref[h] = m_new
            xn = num[0:ML_DV, :]
            xc = xn - jnp.mean(xn, axis=0, keepdims=True)
            var = jnp.mean(xc * xc, axis=0, keepdims=True)
            scale = rden * lax.rsqrt(var * rden * rden + LN_EPS)
            hn = xc * scale * nw[h * ML_DV:(h + 1) * ML_DV, :]
            gate = ogt_ref[c, h * ML_DV:(h + 1) * ML_DV, :].astype(F32)
            out_ref[rows, h * ML_DV:(h + 1) * ML_DV] = (gate * hn).T.astype(BF16)
        return carry

    lax.fori_loop(0, seq // L, chunk, 0)


def _mlstm(qk, mvt, ogt, gif, gt, nw_lanes, batch, seq):
    t, d = qk.shape
    L = ML_CHUNK
    nc = seq // L
    vrows = ML_DV + ML_ONES_ROWS
    seq_blk = lambda w: pl.BlockSpec((seq, w), lambda b: (b, 0))
    slab_blk = lambda r: pl.BlockSpec((nc, r, L), lambda b: (b, 0, 0))
    return pl.pallas_call(
        _mlstm_kernel,
        grid=(batch,),
        in_specs=[seq_blk(d), slab_blk(_MVT_ROWS), slab_blk(d), seq_blk(V7X_LANES),
                  slab_blk(_GT_ROWS), _whole_vmem()],
        out_specs=seq_blk(d),
        out_shape=jax.ShapeDtypeStruct((t, d), BF16),
        scratch_shapes=[pltpu.VMEM((ML_HEADS, vrows, ML_DQK), F32),
                        pltpu.VMEM((ML_HEADS, 1, 1), F32),
                        pltpu.VMEM((ML_HEADS, L, L), F32),
                        pltpu.VMEM((ML_HEADS, vrows, L), F32)],
        compiler_params=_params(1),
        name="mlstm",
    )(qk, mvt, ogt, gif, gt, nw_lanes)


def _diff_attn_kernel(lq1_ref, lk1_ref, lq2_ref, lk2_ref, q_ref, k_ref, vt_ref, nw_ref,
                      o_ref, q12_ref, st_ref, m_ref, acc_ref, *, lam_init):
    tq = q_ref.shape[0]
    tk = ATT_TILE
    heads = q_ref.shape[1] // DA_DV
    i = pl.program_id(2)
    lam = (jnp.exp(jnp.sum(lq1_ref[...] * lk1_ref[...], axis=1, keepdims=True))
           - jnp.exp(jnp.sum(lq2_ref[...] * lk2_ref[...], axis=1, keepdims=True)) + lam_init)

    feat = lax.broadcasted_iota(jnp.int32, (DA_DV, tq), 0)
    first = (feat & (DA_DHEAD // 2)) == 0
    for h in range(heads):
        qt = q_ref[:, h * DA_DV:(h + 1) * DA_DV].astype(F32).T
        q12_ref[h, :, 0:tq] = jnp.where(first, qt, 0.0).astype(BF16)
        q12_ref[h, :, tq:2 * tq] = jnp.where(first, 0.0, qt).astype(BF16)

    vrows = DA_DV + ATT_ONES_ROWS

    def scores(j, h):
        rows = pl.ds(pl.multiple_of(j * tk, tk), tk)
        st_ref[h] = _dot(k_ref[rows, h * DA_DV:(h + 1) * DA_DV], q12_ref[h])

    def softmax(h, mask_add):
        st = st_ref[h]
        if mask_add is not None:
            st = st + mask_add
        m_prev = m_ref[h]
        m_new = jnp.maximum(m_prev, jnp.max(st, axis=0, keepdims=True))
        alpha = jnp.exp2(m_prev - m_new)
        pt = jnp.exp2(st - m_new)
        m_ref[h] = m_new
        return alpha, pt.astype(BF16)

    def accumulate(j, h, alpha, pt):
        acc_ref[h] = alpha * acc_ref[h] + _dot(vt_ref[j, h * vrows:(h + 1) * vrows, :], pt)

    ahead = ATT_SCORE_LOOKAHEAD
    assert ahead <= heads
    for h in range(ahead):
        scores(0, h)
    m_ref[...] = jnp.full_like(m_ref, -jnp.inf)
    acc_ref[...] = jnp.zeros_like(acc_ref)

    def body(j, carry):
        for h in range(heads):
            alpha, pt = softmax(h, None)
            if h + ahead < heads:
                scores(j, h + ahead)
            else:
                scores(j + 1, h + ahead - heads)
            accumulate(j, h, alpha, pt)
        return carry

    lax.fori_loop(0, i, body, 0)
    kj = lax.broadcasted_iota(jnp.int32, (tk, 2 * tq), 0)
    qi = lax.broadcasted_iota(jnp.int32, (tk, 2 * tq), 1) & (tq - 1)
    causal_add = jnp.where(kj <= qi, 0.0, -jnp.inf)
    for h in range(heads):
        alpha, pt = softmax(h, causal_add)
        if h + ahead < heads:
            scores(i, h + ahead)
        accumulate(i, h, alpha, pt)

    for h in range(heads):
        acc = acc_ref[h, 0:DA_DV, :]
        r = 1.0 / acc_ref[h, DA_DV:DA_DV + 1, :]
        ot = acc[:, 0:tq] * r[:, 0:tq] - acc[:, tq:2 * tq] * (lam * r[:, tq:2 * tq])
        ms = jnp.mean(ot * ot, axis=0, keepdims=True)
        o = (ot * (lax.rsqrt(ms + LN_EPS) * (1.0 - lam_init))).T
        o_ref[:, h * DA_DV:(h + 1) * DA_DV] = (
            o * nw_ref[:, h * DA_DV:(h + 1) * DA_DV]).astype(BF16)


def _diff_attn(dq, dk, dvt, lam_vecs, da_norm_w, batch, seq, lam_init):
    t, d = dq.shape
    tq = ATT_TILE
    assert tq & (tq - 1) == 0
    nq = seq // tq
    hb = ATT_HEADS_PER_STEP
    w = hb * DA_DV
    vrows = DA_DV + ATT_ONES_ROWS
    lam_spec = pl.BlockSpec((1, DA_DHEAD), lambda b, g, i: (0, 0))
    q_spec = pl.BlockSpec((tq, w), lambda b, g, i: (b * nq + i, g))
    return pl.pallas_call(
        functools.partial(_diff_attn_kernel, lam_init=lam_init),
        grid=(batch, DA_HEADS // hb, nq),
        in_specs=[lam_spec] * 4 + [q_spec,
                                   pl.BlockSpec((seq, w), lambda b, g, i: (b, g)),
                                   pl.BlockSpec((nq, hb * vrows, tq), lambda b, g, i: (b, g, 0)),
                                   pl.BlockSpec((1, w), lambda b, g, i: (0, g))],
        out_specs=q_spec,
        out_shape=jax.ShapeDtypeStruct((t, d), BF16),
        scratch_shapes=[pltpu.VMEM((hb, DA_DV, 2 * tq), BF16),
                        pltpu.VMEM((hb, tq, 2 * tq), F32),
                        pltpu.VMEM((hb, 1, 2 * tq), F32),
                        pltpu.VMEM((hb, vrows, 2 * tq), F32)],
        compiler_params=_params(3),
        name="diff_attn",
    )(*lam_vecs, dq, dk, dvt, da_norm_w)


def _row_subs(tm):
    return [slice(r * ROW_SUB, (r + 1) * ROW_SUB) for r in range(tm // ROW_SUB)]


def _merge_kernel(x_ref, ha_ref, ob_ref, gate_ref, wa_ref, wb_ref, wm_ref, g_ref, b_ref, o_ref):
    d = x_ref.shape[1]
    subs = _row_subs(x_ref.shape[0])
    yab = [(_dot(ha_ref[s, :], wa_ref[...]), _dot(ob_ref[s, :], wb_ref[...])) for s in subs]
    mix = []
    for s, (ya, yb) in zip(subs, yab):
        mixin = gate_ref[s, 0:d].astype(F32) * ya + gate_ref[s, d:2 * d].astype(F32) * yb
        mix.append(_dot(mixin.astype(BF16), wm_ref[...]))
    for s, mx in zip(subs, mix):
        o_ref[s, :] = _layer_norm(ALPHA * x_ref[s, :] + mx, g_ref[...], b_ref[...])


def _merge(x2, ha, ob, gates, wa, wb, wm, g, b):
    t, d = x2.shape
    tm = ROW_TILE
    row_blk = lambda w: pl.BlockSpec((tm, w), lambda i: (i, 0))
    return pl.pallas_call(
        _merge_kernel,
        grid=(t // tm,),
        in_specs=[row_blk(d), row_blk(d), row_blk(d), row_blk(2 * d)] + [_whole_vmem()] * 5,
        out_specs=row_blk(d),
        out_shape=jax.ShapeDtypeStruct((t, d), F32),
        compiler_params=_params(1),
        name="merge_ln1",
    )(x2, ha, ob, gates, wa, wb, wm, g, b)


def _xattn_kv_kernel(mem_ref, wk_ref, wv_ref, k_ref, v_ref):
    mb = mem_ref[...].astype(BF16)
    k_ref[...] = _dot(mb, wk_ref[...]).astype(BF16)
    v_ref[...] = _dot(mb, wv_ref[...]).astype(BF16)


def _xattn_kv(mem2, wk, wv, mem_len):
    t, d = mem2.shape
    blk = pl.BlockSpec((mem_len, d), lambda b: (b, 0))
    return pl.pallas_call(
        _xattn_kv_kernel,
        grid=(t // mem_len,),
        in_specs=[blk, _whole_vmem(), _whole_vmem()],
        out_specs=[blk, blk],
        out_shape=[jax.ShapeDtypeStruct((t, d), BF16)] * 2,
        compiler_params=_params(1),
        name="xattn_kv",
    )(mem2, wk, wv)


def _xattn_kernel(x_ref, k_ref, v_ref, wq_ref, wo_ref, g_ref, b_ref, o_ref):
    d = x_ref.shape[1]
    dh = d // XA_HEADS
    subs = _row_subs(x_ref.shape[0])
    xq = [(_dot(x_ref[s, :].astype(BF16), wq_ref[...]) * (dh ** -0.5)).astype(BF16) for s in subs]
    xo = []
    for q in xq:
        heads = []
        for h in range(XA_HEADS):
            cols = slice(h * dh, (h + 1) * dh)
            sc = _dot_nt(q[:, cols], k_ref[:, cols])
            e = jnp.exp(sc - jnp.max(sc, axis=1, keepdims=True))
            p = e / jnp.sum(e, axis=1, keepdims=True)
            heads.append(_dot(p.astype(BF16), v_ref[:, cols]).astype(BF16))
        xo.append(_dot(jnp.concatenate(heads, axis=1), wo_ref[...]))
    for s, o in zip(subs, xo):
        o_ref[s, :] = _layer_norm(ALPHA * x_ref[s, :] + o, g_ref[...], b_ref[...])


def _xattn(x1, kx, vx, wq, wo, g, b, seq, mem_len):
    t, d = x1.shape
    tm = ROW_TILE
    tiles_per_seq = seq // tm
    row_blk = pl.BlockSpec((tm, d), lambda i: (i, 0))
    kv_blk = pl.BlockSpec((mem_len, d), lambda i: (i // tiles_per_seq, 0))
    return pl.pallas_call(
        _xattn_kernel,
        grid=(t // tm,),
        in_specs=[row_blk, kv_blk, kv_blk] + [_whole_vmem()] * 4,
        out_specs=row_blk,
        out_shape=jax.ShapeDtypeStruct((t, d), F32),
        compiler_params=_params(1),
        name="xattn_ln2",
    )(x1, kx, vx, wq, wo, g, b)


def _ffn_kernel(x_ref, wg_ref, wu_ref, wd_ref, g_ref, b_ref, o_ref):
    subs = _row_subs(x_ref.shape[0])
    gu = []
    for s in subs:
        xb = x_ref[s, :].astype(BF16)
        gu.append((_dot(xb, wg_ref[...]), _dot(xb, wu_ref[...])))
    down = [_dot((gt * jax.nn.sigmoid(gt) * up).astype(BF16), wd_ref[...]) for gt, up in gu]
    for s, dn in zip(subs, down):
        o_ref[s, :] = _layer_norm(ALPHA * x_ref[s, :] + dn, g_ref[...], b_ref[...])


def _ffn(x2, wg, wu, wd, g, b):
    t, d = x2.shape
    tm = ROW_TILE
    row_blk = pl.BlockSpec((tm, d), lambda i: (i, 0))
    return pl.pallas_call(
        _ffn_kernel,
        grid=(t // tm,),
        in_specs=[row_blk] + [_whole_vmem()] * 5,
        out_specs=row_blk,
        out_shape=jax.ShapeDtypeStruct((t, d), F32),
        compiler_params=_params(1),
        name="ffn_ln3",
    )(x2, wg, wu, wd, g, b)


def _rope_lane_order(a):
    lead = a.shape[:-1]
    a = a.reshape(lead + (DA_HEADS, 2, 2, DA_DHEAD // 2))
    return jnp.swapaxes(a, -3, -2).reshape(lead + (-1,))


def kernel(x, mem, positions, w_in, b_in, conv_w, conv_b, ml_norm_w, lam_q1, lam_k1, lam_q2, lam_k2, da_norm_w, w_proj_a, w_proj_b, w_mix_out, ln1_g, ln1_b, w_xq, w_xk, w_xv, w_xo, ln2_g, ln2_b, w_ffn_gate, w_ffn_up, w_ffn_down, ln3_g, ln3_b):
    batch, seq, d = x.shape
    mem_len = mem.shape[1]
    depth = w_in.shape[0]
    assert depth == DEPTH and seq % ROW_TILE == 0 and seq % ML_CHUNK == 0 and seq % ATT_TILE == 0
    t = batch * seq

    ml_qk_w = 2 * ML_HEADS * ML_DQK
    ml_v_w = ML_HEADS * ML_DV
    da_w = DA_HEADS * DA_DV
    sizes = (ml_qk_w, ml_v_w, ml_v_w, 2 * ML_HEADS, da_w, da_w, da_w, 2 * d)
    offs = np.concatenate([[0], np.cumsum(sizes)])
    assert ml_qk_w == d and ml_v_w == d and da_w == d and offs[-1] == w_in.shape[2]
    half = DA_DHEAD // 2
    invf = ROPE_THETA ** (-jnp.arange(half, dtype=F32) / half)
    invf = jnp.tile(invf, V7X_LANES // half).reshape(1, V7X_LANES)

    x2 = x.reshape(t, d)
    pos2 = jnp.repeat(positions.reshape(t // SLAB, ROPE_PACK, SLAB // ROPE_PACK).transpose(0, 2, 1),
                      V7X_LANES // ROPE_PACK, axis=2).reshape(t // ROPE_PACK, V7X_LANES)
    mem2 = mem.reshape(batch * mem_len, d)

    for l in range(depth):
        def piece(a, j):
            return a[..., offs[j]:offs[j + 1]]

        def pack(a):
            pad = jnp.zeros(a.shape[:-1] + (V7X_LANES - 2 * ML_HEADS,), a.dtype)
            return jnp.concatenate(
                [piece(a, 0), _rope_lane_order(piece(a, 4)), _rope_lane_order(piece(a, 5)),
                 piece(a, 7), piece(a, 3), pad], axis=-1)

        w_bf = w_in[l].astype(BF16)
        w_packed = pack(w_bf)
        b_packed = pack(b_in[l].reshape(1, -1))
        assert w_packed.shape[1] == _W_PACKED

        def with_ones(j, heads, dv, ones):
            wt = jnp.pad(piece(w_bf, j).T.reshape(heads, dv, d), ((0, 0), (0, ones), (0, 0)))
            bt = jnp.pad(piece(b_in[l], j).reshape(heads, dv), ((0, 0), (0, ones)),
                         constant_values=1.0)
            return wt.reshape(-1, d), bt.reshape(-1)

        w_mvt, b_mvt = with_ones(1, ML_HEADS, ML_DV, ML_ONES_ROWS)
        w_dvt, b_dvt = with_ones(6, DA_HEADS, DA_DV, ATT_ONES_ROWS)
        gpad = _GT_ROWS - 2 * ML_HEADS
        w_t = jnp.concatenate([w_mvt, piece(w_bf, 2).T, w_dvt,
                               jnp.pad(piece(w_bf, 3).T, ((0, gpad), (0, 0)))], axis=0)
        b_t = jnp.concatenate([b_mvt, piece(b_in[l], 2), b_dvt,
                               jnp.pad(piece(b_in[l], 3), (0, gpad))]).reshape(-1, 1)
        assert w_t.shape[0] == _WT_ROWS

        qk, dq, dk, gates, gif, mvt, ogt, dvt, gt = _in_proj(
            x2, pos2, w_packed, b_packed, conv_w[l], conv_b[l].reshape(1, -1), invf,
            w_t, b_t, seq)

        nw_lanes = jnp.broadcast_to(ml_norm_w[l].reshape(-1, 1), (ML_HEADS * ML_DV, V7X_LANES))
        ha = _mlstm(qk, mvt, ogt, gif, gt, nw_lanes, batch, seq)

        lam_init = 0.8 - 0.6 * math.exp(-0.3 * l)
        lam_vecs = [a[l].reshape(1, -1) for a in (lam_q1, lam_k1, lam_q2, lam_k2)]
        ob = _diff_attn(dq, dk, dvt, lam_vecs, da_norm_w[l].reshape(1, -1), batch, seq, lam_init)

        x2 = _merge(x2, ha, ob, gates, w_proj_a[l].astype(BF16), w_proj_b[l].astype(BF16),
                    w_mix_out[l].astype(BF16), ln1_g[l].reshape(1, -1), ln1_b[l].reshape(1, -1))

        kx, vx = _xattn_kv(mem2, w_xk[l].astype(BF16), w_xv[l].astype(BF16), mem_len)
        x2 = _xattn(x2, kx, vx, w_xq[l].astype(BF16), w_xo[l].astype(BF16),
                    ln2_g[l].reshape(1, -1), ln2_b[l].reshape(1, -1), seq, mem_len)

        x2 = _ffn(x2, w_ffn_gate[l].astype(BF16), w_ffn_up[l].astype(BF16),
                  w_ffn_down[l].astype(BF16), ln3_g[l].reshape(1, -1), ln3_b[l].reshape(1, -1))

    return x2.reshape(batch, seq, d)
```

```python
import functools
import math

import numpy as np
import jax
import jax.numpy as jnp
from jax import lax
from jax.experimental import pallas as pl
from jax.experimental.pallas import tpu as pltpu

ML_HEADS = 4
ML_DQK = 128
ML_DV = 256
ML_CONV = 4
DA_HEADS = 8
DA_DHEAD = 64
DA_DV = 2 * DA_DHEAD
XA_HEADS = 4
ROPE_THETA = 10000.0
DEPTH = 1
ALPHA = (2.0 * DEPTH) ** 0.25
LN_EPS = 1e-5

V7X_LANES = 128
V7X_SUBLANES = 8
V7X_VMEM_LIMIT = 56 * 1024 * 1024

ROW_TILE = 512
ROW_SUB = 256
SLAB = 256
ML_CHUNK = SLAB
ML_ONES_ROWS = 16
ATT_TILE = SLAB
ATT_HEADS_PER_STEP = 8
ATT_SCORE_LOOKAHEAD = 2
ATT_ONES_ROWS = 16
LOG2_E = math.log2(math.e)
ROPE_PACK = V7X_LANES // (DA_DHEAD // 2)

F32 = jnp.float32
BF16 = jnp.bfloat16


def _dot(a, b):
    return jnp.dot(a, b, preferred_element_type=F32)


def _dot_nt(a, b):
    return lax.dot_general(a, b, (((1,), (1,)), ((), ())), preferred_element_type=F32)


def _dot_tn(a, b):
    return lax.dot_general(a, b, (((0,), (0,)), ((), ())), preferred_element_type=F32)


def _layer_norm(z, g, b):
    mu = jnp.mean(z, axis=-1, keepdims=True)
    zc = z - mu
    var = jnp.mean(zc * zc, axis=-1, keepdims=True)
    return zc * lax.rsqrt(var + LN_EPS) * g + b


def _whole_vmem():
    return pl.BlockSpec(memory_space=pltpu.VMEM)


def _params(n_axes):
    return pltpu.CompilerParams(dimension_semantics=("arbitrary",) * n_axes,
                                vmem_limit_bytes=V7X_VMEM_LIMIT)


_C_QK, _C_DQ, _C_DK, _C_GATE, _C_IF = (0, 1024, 2048, 3072, 5120)
_W_PACKED = _C_IF + V7X_LANES
_MVT_ROWS = ML_HEADS * (ML_DV + ML_ONES_ROWS)
_DVT_ROWS = DA_HEADS * (DA_DV + ATT_ONES_ROWS)
_GT_ROWS = 16
_R_MV, _R_OG, _R_DV, _R_GT = 0, _MVT_ROWS, _MVT_ROWS + 1024, _MVT_ROWS + 1024 + _DVT_ROWS
_WT_ROWS = _R_GT + _GT_ROWS


def _in_proj_kernel(x_ref, pos_ref, w_ref, b_ref, cw_ref, cb_ref, invf_ref, wt_ref, bt_ref,
                    qk_ref, dq_ref, dk_ref, gate_ref, gif_ref, mvt_ref, ogt_ref, dvt_ref, gt_ref,
                    pqk_ref, pdq_ref, pdk_ref, pg1_ref, pg2_ref, rt_ref, *, tiles_per_seq):
    tm = x_ref.shape[0]
    d = qk_ref.shape[1]
    half = d // 2

    halo = V7X_SUBLANES
    bqk = b_ref[:, _C_QK:_C_QK + d]

    @pl.when(pl.program_id(0) % tiles_per_seq == 0)
    def _():
        pqk_ref[0:halo, :] = jnp.broadcast_to(-bqk, (halo, d))

    taps = [cw_ref[j:j + 1, :] for j in range(ML_CONV)]
    conv_bias = cb_ref[...] + bqk * (taps[0] + taps[1] + taps[2] + taps[3])
    lane = lax.broadcasted_iota(jnp.int32, (1, V7X_LANES), 1)
    rope_w = V7X_LANES // ROPE_PACK

    n_sub = tm // SLAB

    def proj_to(dst_ref, xb, c0):
        dst_ref[...] = _dot(xb, w_ref[:, c0:c0 + d])

    def bias(c0, width):
        return b_ref[:, c0:c0 + width]

    xb = x_ref[0:SLAB, :].astype(BF16)
    proj_to(pqk_ref.at[halo:halo + SLAB], xb, _C_QK)
    for c in range(n_sub):
        rs = slice(c * SLAB, (c + 1) * SLAB)
        proj_to(pdq_ref, xb, _C_DQ)
        proj_to(pdk_ref, xb, _C_DK)
        rt_ref[...] = _dot_nt(wt_ref[...], xb)

        acc = conv_bias
        for j in range(ML_CONV):
            back = ML_CONV - 1 - j
            acc = acc + pqk_ref[halo - back:halo - back + SLAB, :] * taps[j]
        pqk_ref[0:halo, :] = pqk_ref[SLAB:SLAB + halo, :]
        y = acc * jax.nn.sigmoid(acc)
        qk_ref[rs, 0:half] = (y[:, 0:half] * (ML_DQK ** -0.5)).astype(BF16)
        qk_ref[rs, half:d] = y[:, half:d].astype(BF16)

        proj_to(pg1_ref, xb, _C_GATE)
        proj_to(pg2_ref, xb, _C_GATE + d)
        gif_ref[rs, :] = _dot(xb, w_ref[:, _C_IF:_C_IF + V7X_LANES]) + bias(_C_IF, V7X_LANES)

        pk = SLAB // ROPE_PACK
        ang = pos_ref[c * pk:(c + 1) * pk, :].astype(F32) * invf_ref[...]
        cos_p = jnp.cos(ang)
        sin_p = jnp.sin(ang)

        def spread(v):
            out = []
            for g in range(ROPE_PACK):
                u = v if g == 0 else pltpu.roll(v, V7X_LANES - g * rope_w, 1)
                u = jnp.where(lane < rope_w, u, pltpu.roll(u, rope_w, 1))
                out.append(jnp.where(lane < 2 * rope_w, u, pltpu.roll(u, 2 * rope_w, 1)))
            return jnp.concatenate(out, axis=0)

        cosv = spread(cos_p)
        sinv = spread(sin_p)
        ssin = jnp.where(lane < V7X_LANES // 2, -sinv, sinv)

        def rope_store(out_ref, p_ref, c0, scale):
            for h in range(d // V7X_LANES):
                cols = slice(h * V7X_LANES, (h + 1) * V7X_LANES)
                xh = p_ref[:, cols] + b_ref[:, c0 + h * V7X_LANES:c0 + (h + 1) * V7X_LANES]
                r = xh * cosv + pltpu.roll(xh, V7X_LANES // 2, 1) * ssin
                out_ref[rs, cols] = (r * scale).astype(BF16)

        rope_store(dq_ref, pdq_ref, _C_DQ, DA_DHEAD ** -0.5 * LOG2_E)
        rope_store(dk_ref, pdk_ref, _C_DK, 1.0)

        mvt_ref[c] = (rt_ref[_R_MV:_R_OG, :] + bt_ref[_R_MV:_R_OG, :]).astype(BF16)
        ogt_ref[c] = jax.nn.sigmoid(rt_ref[_R_OG:_R_DV, :] + bt_ref[_R_OG:_R_DV, :]).astype(BF16)
        dvt_ref[c] = (rt_ref[_R_DV:_R_GT, :] + bt_ref[_R_DV:_R_GT, :]).astype(BF16)
        gt_ref[c] = rt_ref[_R_GT:_WT_ROWS, :] + bt_ref[_R_GT:_WT_ROWS, :]

        if c + 1 < n_sub:
            xb = x_ref[(c + 1) * SLAB:(c + 2) * SLAB, :].astype(BF16)
            proj_to(pqk_ref.at[halo:halo + SLAB], xb, _C_QK)

        gate_ref[rs, 0:d] = jax.nn.sigmoid(pg1_ref[...] + bias(_C_GATE, d)).astype(BF16)
        gate_ref[rs, d:2 * d] = jax.nn.sigmoid(pg2_ref[...] + bias(_C_GATE + d, d)).astype(BF16)


def _in_proj(x2, pos2, w_packed, b_packed, conv_w, conv_b, invf, w_t, b_t, seq):
    t, d = x2.shape
    tm = ROW_TILE
    row_blk = lambda w: pl.BlockSpec((tm, w), lambda i: (i, 0))
    slab_blk = lambda r: pl.BlockSpec((tm // SLAB, r, SLAB), lambda i: (i, 0, 0))
    row_out = lambda w, dt: jax.ShapeDtypeStruct((t, w), dt)
    slab_out = lambda r, dt: jax.ShapeDtypeStruct((t // SLAB, r, SLAB), dt)
    return pl.pallas_call(
        functools.partial(_in_proj_kernel, tiles_per_seq=seq // tm),
        grid=(t // tm,),
        in_specs=[row_blk(d), pl.BlockSpec((tm // ROPE_PACK, V7X_LANES), lambda i: (i, 0))]
        + [_whole_vmem()] * 7,
        out_specs=[row_blk(d), row_blk(d), row_blk(d), row_blk(2 * d), row_blk(V7X_LANES),
                   slab_blk(_MVT_ROWS), slab_blk(d), slab_blk(_DVT_ROWS), slab_blk(_GT_ROWS)],
        out_shape=[row_out(d, BF16), row_out(d, BF16), row_out(d, BF16), row_out(2 * d, BF16),
                   row_out(V7X_LANES, F32), slab_out(_MVT_ROWS, BF16), slab_out(d, BF16),
                   slab_out(_DVT_ROWS, BF16), slab_out(_GT_ROWS, F32)],
        scratch_shapes=[pltpu.VMEM((V7X_SUBLANES + SLAB, d), F32)]
        + [pltpu.VMEM((SLAB, d), F32)] * 4 + [pltpu.VMEM((_WT_ROWS, SLAB), F32)],
        compiler_params=_params(1),
        name="in_proj",
    )(x2, pos2, w_packed, b_packed, conv_w, conv_b, invf, w_t, b_t)


def _split3_bf16(a):
    hi = a.astype(BF16)
    r1 = a - hi.astype(F32)
    mid = r1.astype(BF16)
    lo = (r1 - mid.astype(F32)).astype(BF16)
    return hi, mid, lo


def _mlstm_kernel(qk_ref, vt_ref, ogt_ref, gif_ref, gt_ref, nw_ref, out_ref,
                  c_ref, m_ref, qkt_ref, cq_ref):
    seq = qk_ref.shape[0]
    L = ML_CHUNK
    vrows = ML_DV + ML_ONES_ROWS
    c_ref[...] = jnp.zeros_like(c_ref)
    m_ref[...] = jnp.zeros_like(m_ref)

    si = lax.broadcasted_iota(jnp.int32, (L, L), 0)
    ti = lax.broadcasted_iota(jnp.int32, (L, L), 1)
    tri_lo = jnp.where(ti <= si, 1.0, 0.0).astype(BF16)
    tri_up = jnp.where(si <= ti, 1.0, 0.0).astype(BF16)
    causal_t = si <= ti
    nw = jnp.concatenate([nw_ref[...]] * (L // V7X_LANES), axis=1)

    def chunk(c, carry):
        rows = pl.ds(pl.multiple_of(c * L, L), L)
        for h in range(ML_HEADS):
            q = qk_ref[rows, h * ML_DQK:(h + 1) * ML_DQK]
            k = qk_ref[rows, (ML_HEADS + h) * ML_DQK:(ML_HEADS + h + 1) * ML_DQK]
            qkt_ref[h] = _dot_nt(k, q)
            cq_ref[h] = _dot_nt(c_ref[h].astype(BF16), q)
        gt = gt_ref[c]
        r3 = _split3_bf16(jax.nn.log_sigmoid(gt))
        bc_row = _dot(r3[0], tri_up) + _dot(r3[1], tri_up) + _dot(r3[2], tri_up)
        g = gif_ref[rows, :]
        c3 = _split3_bf16(jax.nn.log_sigmoid(g))
        bc_col = _dot(tri_lo, c3[0]) + _dot(tri_lo, c3[1]) + _dot(tri_lo, c3[2])
        z = g - pltpu.roll(bc_col, V7X_LANES - ML_HEADS, 1)
        for h in range(ML_HEADS):
            b_row = bc_row[ML_HEADS + h:ML_HEADS + h + 1, :]
            i_row = gt[h:h + 1, :]
            r_col = z[:, h:h + 1]
            m_prev = m_ref[h]
            dmat = jnp.where(causal_t, b_row + r_col, -jnp.inf)
            inter = b_row + m_prev
            m_row = jnp.maximum(inter, jnp.max(dmat, axis=0, keepdims=True))
            w_inter = jnp.exp(inter - m_row)
            pt = (qkt_ref[h] * jnp.exp(dmat - m_row)).astype(BF16)
            vt = vt_ref[c, h * vrows:(h + 1) * vrows, :]
            num = w_inter * cq_ref[h] + _dot(vt, pt)
            den = num[ML_DV:ML_DV + 1, :]
            rden = 1.0 / jnp.maximum(jnp.abs(den), jnp.exp(-m_row))
            b_last = b_row[:, L - 1:L]
            g_row = b_last - b_row + i_row
            m_new = jnp.maximum(b_last + m_prev, jnp.max(g_row, axis=1, keepdims=True))
            decay = jnp.exp(b_last + m_prev - m_new)
            ws = jnp.exp(g_row - m_new)
            k = qk_ref[rows, (ML_HEADS + h) * ML_DQK:(ML_HEADS + h + 1) * ML_DQK]
            c_ref[h] = decay * c_ref[h] + _dot((vt.astype(F32) * ws).astype(BF16), k)
            m_ref[h] = m_new
            xn = num[0:ML_DV, :]
            xc = xn - jnp.mean(xn, axis=0, keepdims=True)
            var = jnp.mean(xc * xc, axis=0, keepdims=True)
            scale = rden * lax.rsqrt(var * rden * rden + LN_EPS)
            hn = xc * scale * nw[h * ML_DV:(h + 1) * ML_DV, :]
            gate = ogt_ref[c, h * ML_DV:(h + 1) * ML_DV, :].astype(F32)
            out_ref[rows, h * ML_DV:(h + 1) * ML_DV] = (gate * hn).T.astype(BF16)
        return carry

    lax.fori_loop(0, seq // L, chunk, 0)


def _mlstm(qk, mvt, ogt, gif, gt, nw_lanes, batch, seq):
    t, d = qk.shape
    L = ML_CHUNK
    nc = seq // L
    vrows = ML_DV + ML_ONES_ROWS
    seq_blk = lambda w: pl.BlockSpec((seq, w), lambda b: (b, 0))
    slab_blk = lambda r: pl.BlockSpec((nc, r, L), lambda b: (b, 0, 0))
    return pl.pallas_call(
        _mlstm_kernel,
        grid=(batch,),
        in_specs=[seq_blk(d), slab_blk(_MVT_ROWS), slab_blk(d), seq_blk(V7X_LANES),
                  slab_blk(_GT_ROWS), _whole_vmem()],
        out_specs=seq_blk(d),
        out_shape=jax.ShapeDtypeStruct((t, d), BF16),
        scratch_shapes=[pltpu.VMEM((ML_HEADS, vrows, ML_DQK), F32),
                        pltpu.VMEM((ML_HEADS, 1, 1), F32),
                        pltpu.VMEM((ML_HEADS, L, L), F32),
                        pltpu.VMEM((ML_HEADS, vrows, L), F32)],
        compiler_params=_params(1),
        name="mlstm",
    )(qk, mvt, ogt, gif, gt, nw_lanes)


def _diff_attn_kernel(lq1_ref, lk1_ref, lq2_ref, lk2_ref, q_ref, k_ref, vt_ref, nw_ref,
                      o_ref, q12_ref, st_ref, mx_ref, m_ref, acc_ref, *, lam_init):
    tq = q_ref.shape[0]
    tk = ATT_TILE
    heads = q_ref.shape[1] // DA_DV
    i = pl.program_id(2)
    lam = (jnp.exp(jnp.sum(lq1_ref[...] * lk1_ref[...], axis=1, keepdims=True))
           - jnp.exp(jnp.sum(lq2_ref[...] * lk2_ref[...], axis=1, keepdims=True)) + lam_init)

    feat = lax.broadcasted_iota(jnp.int32, (DA_DV, tq), 0)
    first = (feat & (DA_DHEAD // 2)) == 0
    for h in range(heads):
        qt = q_ref[:, h * DA_DV:(h + 1) * DA_DV].astype(F32).T
        q12_ref[h, :, 0:tq] = jnp.where(first, qt, 0.0).astype(BF16)
        q12_ref[h, :, tq:2 * tq] = jnp.where(first, 0.0, qt).astype(BF16)

    vrows = DA_DV + ATT_ONES_ROWS

    def scores(j, h):
        rows = pl.ds(pl.multiple_of(j * tk, tk), tk)
        st = _dot(k_ref[rows, h * DA_DV:(h + 1) * DA_DV], q12_ref[h])
        st_ref[h] = st
        mx_ref[h] = jnp.max(st, axis=0, keepdims=True)

    def softmax(h, mask_add):
        st = st_ref[h]
        if mask_add is None:
            mx = mx_ref[h]
        else:
            st = st + mask_add
            mx = jnp.max(st, axis=0, keepdims=True)
        m_prev = m_ref[h]
        m_new = jnp.maximum(m_prev, mx)
        alpha = jnp.exp2(m_prev - m_new)
        pt = jnp.exp2(st - m_new)
        m_ref[h] = m_new
        return alpha, pt.astype(BF16)

    def accumulate(j, h, alpha, pt):
        acc_ref[h] = alpha * acc_ref[h] + _dot(vt_ref[j, h * vrows:(h + 1) * vrows, :], pt)

    ahead = ATT_SCORE_LOOKAHEAD
    assert ahead <= heads
    for h in range(ahead):
        scores(0, h)
    m_ref[...] = jnp.full_like(m_ref, -jnp.inf)
    acc_ref[...] = jnp.zeros_like(acc_ref)

    def body(j, carry):
        for h in range(heads):
            alpha, pt = softmax(h, None)
            if h + ahead < heads:
                scores(j, h + ahead)
            else:
                scores(j + 1, h + ahead - heads)
            accumulate(j, h, alpha, pt)
        return carry

    lax.fori_loop(0, i, body, 0)
    kj = lax.broadcasted_iota(jnp.int32, (tk, 2 * tq), 0)
    qi = lax.broadcasted_iota(jnp.int32, (tk, 2 * tq), 1) & (tq - 1)
    causal_add = jnp.where(kj <= qi, 0.0, -jnp.inf)
    for h in range(heads):
        alpha, pt = softmax(h, causal_add)
        if h + ahead < heads:
            scores(i, h + ahead)
        accumulate(i, h, alpha, pt)

    for h in range(heads):
        acc = acc_ref[h, 0:DA_DV, :]
        r = 1.0 / acc_ref[h, DA_DV:DA_DV + 1, :]
        ot = acc[:, 0:tq] * r[:, 0:tq] - acc[:, tq:2 * tq] * (lam * r[:, tq:2 * tq])
        ms = jnp.mean(ot * ot, axis=0, keepdims=True)
        o = (ot * (lax.rsqrt(ms + LN_EPS) * (1.0 - lam_init))).T
        o_ref[:, h * DA_DV:(h + 1) * DA_DV] = (
            o * nw_ref[:, h * DA_DV:(h + 1) * DA_DV]).astype(BF16)


def _diff_attn(dq, dk, dvt, lam_vecs, da_norm_w, batch, seq, lam_init):
    t, d = dq.shape
    tq = ATT_TILE
    assert tq & (tq - 1) == 0
    nq = seq // tq
    hb = ATT_HEADS_PER_STEP
    w = hb * DA_DV
    vrows = DA_DV + ATT_ONES_ROWS
    lam_spec = pl.BlockSpec((1, DA_DHEAD), lambda b, g, i: (0, 0))
    q_spec = pl.BlockSpec((tq, w), lambda b, g, i: (b * nq + i, g))
    return pl.pallas_call(
        functools.partial(_diff_attn_kernel, lam_init=lam_init),
        grid=(batch, DA_HEADS // hb, nq),
        in_specs=[lam_spec] * 4 + [q_spec,
                                   pl.BlockSpec((seq, w), lambda b, g, i: (b, g)),
                                   pl.BlockSpec((nq, hb * vrows, tq), lambda b, g, i: (b, g, 0)),
                                   pl.BlockSpec((1, w), lambda b, g, i: (0, g))],
        out_specs=q_spec,
        out_shape=jax.ShapeDtypeStruct((t, d), BF16),
        scratch_shapes=[pltpu.VMEM((hb, DA_DV, 2 * tq), BF16),
                        pltpu.VMEM((hb, tq, 2 * tq), F32),
                        pltpu.VMEM((hb, 1, 2 * tq), F32),
                        pltpu.VMEM((hb, 1, 2 * tq), F32),
                        pltpu.VMEM((hb, vrows, 2 * tq), F32)],
        compiler_params=_params(3),
        name="diff_attn",
    )(*lam_vecs, dq, dk, dvt, da_norm_w)


def _row_subs(tm):
    return [slice(r * ROW_SUB, (r + 1) * ROW_SUB) for r in range(tm // ROW_SUB)]


def _merge_kernel(x_ref, ha_ref, ob_ref, gate_ref, wa_ref, wb_ref, wm_ref, g_ref, b_ref, o_ref):
    d = x_ref.shape[1]
    subs = _row_subs(x_ref.shape[0])
    yab = [(_dot(ha_ref[s, :], wa_ref[...]), _dot(ob_ref[s, :], wb_ref[...])) for s in subs]
    mix = []
    for s, (ya, yb) in zip(subs, yab):
        mixin = gate_ref[s, 0:d].astype(F32) * ya + gate_ref[s, d:2 * d].astype(F32) * yb
        mix.append(_dot(mixin.astype(BF16), wm_ref[...]))
    for s, mx in zip(subs, mix):
        o_ref[s, :] = _layer_norm(ALPHA * x_ref[s, :] + mx, g_ref[...], b_ref[...])


def _merge(x2, ha, ob, gates, wa, wb, wm, g, b):
    t, d = x2.shape
    tm = ROW_TILE
    row_blk = lambda w: pl.BlockSpec((tm, w), lambda i: (i, 0))
    return pl.pallas_call(
        _merge_kernel,
        grid=(t // tm,),
        in_specs=[row_blk(d), row_blk(d), row_blk(d), row_blk(2 * d)] + [_whole_vmem()] * 5,
        out_specs=row_blk(d),
        out_shape=jax.ShapeDtypeStruct((t, d), F32),
        compiler_params=_params(1),
        name="merge_ln1",
    )(x2, ha, ob, gates, wa, wb, wm, g, b)


def _xattn_kv_kernel(mem_ref, wk_ref, wv_ref, k_ref, v_ref):
    mb = mem_ref[...].astype(BF16)
    k_ref[...] = _dot(mb, wk_ref[...]).astype(BF16)
    v_ref[...] = _dot(mb, wv_ref[...]).astype(BF16)


def _xattn_kv(mem2, wk, wv, mem_len):
    t, d = mem2.shape
    blk = pl.BlockSpec((mem_len, d), lambda b: (b, 0))
    return pl.pallas_call(
        _xattn_kv_kernel,
        grid=(t // mem_len,),
        in_specs=[blk, _whole_vmem(), _whole_vmem()],
        out_specs=[blk, blk],
        out_shape=[jax.ShapeDtypeStruct((t, d), BF16)] * 2,
        compiler_params=_params(1),
        name="xattn_kv",
    )(mem2, wk, wv)


def _xattn_kernel(x_ref, k_ref, v_ref, wq_ref, wo_ref, g_ref, b_ref, o_ref):
    d = x_ref.shape[1]
    dh = d // XA_HEADS
    subs = _row_subs(x_ref.shape[0])
    xq = [(_dot(x_ref[s, :].astype(BF16), wq_ref[...]) * (dh ** -0.5)).astype(BF16) for s in subs]
    xo = []
    for q in xq:
        heads = []
        for h in range(XA_HEADS):
            cols = slice(h * dh, (h + 1) * dh)
            sc = _dot_nt(q[:, cols], k_ref[:, cols])
            e = jnp.exp(sc - jnp.max(sc, axis=1, keepdims=True))
            p = e / jnp.sum(e, axis=1, keepdims=True)
            heads.append(_dot(p.astype(BF16), v_ref[:, cols]).astype(BF16))
        xo.append(_dot(jnp.concatenate(heads, axis=1), wo_ref[...]))
    for s, o in zip(subs, xo):
        o_ref[s, :] = _layer_norm(ALPHA * x_ref[s, :] + o, g_ref[...], b_ref[...])


def _xattn(x1, kx, vx, wq, wo, g, b, seq, mem_len):
    t, d = x1.shape
    tm = ROW_TILE
    tiles_per_seq = seq // tm
    row_blk = pl.BlockSpec((tm, d), lambda i: (i, 0))
    kv_blk = pl.BlockSpec((mem_len, d), lambda i: (i // tiles_per_seq, 0))
    return pl.pallas_call(
        _xattn_kernel,
        grid=(t // tm,),
        in_specs=[row_blk, kv_blk, kv_blk] + [_whole_vmem()] * 4,
        out_specs=row_blk,
        out_shape=jax.ShapeDtypeStruct((t, d), F32),
        compiler_params=_params(1),
        name="xattn_ln2",
    )(x1, kx, vx, wq, wo, g, b)


def _ffn_kernel(x_ref, wg_ref, wu_ref, wd_ref, g_ref, b_ref, o_ref):
    subs = _row_subs(x_ref.shape[0])
    gu = []
    for s in subs:
        xb = x_ref[s, :].astype(BF16)
        gu.append((_dot(xb, wg_ref[...]), _dot(xb, wu_ref[...])))
    down = [_dot((gt * jax.nn.sigmoid(gt) * up).astype(BF16), wd_ref[...]) for gt, up in gu]
    for s, dn in zip(subs, down):
        o_ref[s, :] = _layer_norm(ALPHA * x_ref[s, :] + dn, g_ref[...], b_ref[...])


def _ffn(x2, wg, wu, wd, g, b):
    t, d = x2.shape
    tm = ROW_TILE
    row_blk = pl.BlockSpec((tm, d), lambda i: (i, 0))
    return pl.pallas_call(
        _ffn_kernel,
        grid=(t // tm,),
        in_specs=[row_blk] + [_whole_vmem()] * 5,
        out_specs=row_blk,
        out_shape=jax.ShapeDtypeStruct((t, d), F32),
        compiler_params=_params(1),
        name="ffn_ln3",
    )(x2, wg, wu, wd, g, b)


def _rope_lane_order(a):
    lead = a.shape[:-1]
    a = a.reshape(lead + (DA_HEADS, 2, 2, DA_DHEAD // 2))
    return jnp.swapaxes(a, -3, -2).reshape(lead + (-1,))


def kernel(x, mem, positions, w_in, b_in, conv_w, conv_b, ml_norm_w, lam_q1, lam_k1, lam_q2, lam_k2, da_norm_w, w_proj_a, w_proj_b, w_mix_out, ln1_g, ln1_b, w_xq, w_xk, w_xv, w_xo, ln2_g, ln2_b, w_ffn_gate, w_ffn_up, w_ffn_down, ln3_g, ln3_b):
    batch, seq, d = x.shape
    mem_len = mem.shape[1]
    depth = w_in.shape[0]
    assert depth == DEPTH and seq % ROW_TILE == 0 and seq % ML_CHUNK == 0 and seq % ATT_TILE == 0
    t = batch * seq

    ml_qk_w = 2 * ML_HEADS * ML_DQK
    ml_v_w = ML_HEADS * ML_DV
    da_w = DA_HEADS * DA_DV
    sizes = (ml_qk_w, ml_v_w, ml_v_w, 2 * ML_HEADS, da_w, da_w, da_w, 2 * d)
    offs = np.concatenate([[0], np.cumsum(sizes)])
    assert ml_qk_w == d and ml_v_w == d and da_w == d and offs[-1] == w_in.shape[2]
    half = DA_DHEAD // 2
    invf = ROPE_THETA ** (-jnp.arange(half, dtype=F32) / half)
    invf = jnp.tile(invf, V7X_LANES // half).reshape(1, V7X_LANES)

    x2 = x.reshape(t, d)
    pos2 = jnp.repeat(positions.reshape(t // SLAB, ROPE_PACK, SLAB // ROPE_PACK).transpose(0, 2, 1),
                      V7X_LANES // ROPE_PACK, axis=2).reshape(t // ROPE_PACK, V7X_LANES)
    mem2 = mem.reshape(batch * mem_len, d)

    for l in range(depth):
        def piece(a, j):
            return a[..., offs[j]:offs[j + 1]]

        def pack(a):
            pad = jnp.zeros(a.shape[:-1] + (V7X_LANES - 2 * ML_HEADS,), a.dtype)
            return jnp.concatenate(
                [piece(a, 0), _rope_lane_order(piece(a, 4)), _rope_lane_order(piece(a, 5)),
                 piece(a, 7), piece(a, 3), pad], axis=-1)

        w_bf = w_in[l].astype(BF16)
        w_packed = pack(w_bf)
        b_packed = pack(b_in[l].reshape(1, -1))
        assert w_packed.shape[1] == _W_PACKED

        def with_ones(j, heads, dv, ones):
            wt = jnp.pad(piece(w_bf, j).T.reshape(heads, dv, d), ((0, 0), (0, ones), (0, 0)))
            bt = jnp.pad(piece(b_in[l], j).reshape(heads, dv), ((0, 0), (0, ones)),
                         constant_values=1.0)
            return wt.reshape(-1, d), bt.reshape(-1)

        w_mvt, b_mvt = with_ones(1, ML_HEADS, ML_DV, ML_ONES_ROWS)
        w_dvt, b_dvt = with_ones(6, DA_HEADS, DA_DV, ATT_ONES_ROWS)
        gpad = _GT_ROWS - 2 * ML_HEADS
        w_t = jnp.concatenate([w_mvt, piece(w_bf, 2).T, w_dvt,
                               jnp.pad(piece(w_bf, 3).T, ((0, gpad), (0, 0)))], axis=0)
        b_t = jnp.concatenate([b_mvt, piece(b_in[l], 2), b_dvt,
                               jnp.pad(piece(b_in[l], 3), (0, gpad))]).reshape(-1, 1)
        assert w_t.shape[0] == _WT_ROWS

        qk, dq, dk, gates, gif, mvt, ogt, dvt, gt = _in_proj(
            x2, pos2, w_packed, b_packed, conv_w[l], conv_b[l].reshape(1, -1), invf,
            w_t, b_t, seq)

        nw_lanes = jnp.broadcast_to(ml_norm_w[l].reshape(-1, 1), (ML_HEADS * ML_DV, V7X_LANES))
        ha = _mlstm(qk, mvt, ogt, gif, gt, nw_lanes, batch, seq)

        lam_init = 0.8 - 0.6 * math.exp(-0.3 * l)
        lam_vecs = [a[l].reshape(1, -1) for a in (lam_q1, lam_k1, lam_q2, lam_k2)]
        ob = _diff_attn(dq, dk, dvt, lam_vecs, da_norm_w[l].reshape(1, -1), batch, seq, lam_init)

        x2 = _merge(x2, ha, ob, gates, w_proj_a[l].astype(BF16), w_proj_b[l].astype(BF16),
                    w_mix_out[l].astype(BF16), ln1_g[l].reshape(1, -1), ln1_b[l].reshape(1, -1))

        kx, vx = _xattn_kv(mem2, w_xk[l].astype(BF16), w_xv[l].astype(BF16), mem_len)
        x2 = _xattn(x2, kx, vx, w_xq[l].astype(BF16), w_xo[l].astype(BF16),
                    ln2_g[l].reshape(1, -1), ln2_b[l].reshape(1, -1), seq, mem_len)

        x2 = _ffn(x2, w_ffn_gate[l].astype(BF16), w_ffn_up[l].astype(BF16),
                  w_ffn_down[l].astype(BF16), ln3_g[l].reshape(1, -1), ln3_b[l].reshape(1, -1))

    return x2.reshape(batch, seq, d)
```

```python
import functools
import math

import numpy as np
import jax
import jax.numpy as jnp
from jax import lax
from jax.experimental import pallas as pl
from jax.experimental.pallas import tpu as pltpu

ML_HEADS = 4
ML_DQK = 128
ML_DV = 256
ML_CONV = 4
DA_HEADS = 8
DA_DHEAD = 64
DA_DV = 2 * DA_DHEAD
XA_HEADS = 4
ROPE_THETA = 10000.0
DEPTH = 1
ALPHA = (2.0 * DEPTH) ** 0.25
LN_EPS = 1e-5

V7X_LANES = 128
V7X_SUBLANES = 8
V7X_VMEM_LIMIT = 56 * 1024 * 1024

ROW_TILE = 512
ROW_SUB = 256
SLAB = 256
ML_CHUNK = SLAB
ML_ONES_ROWS = 16
ATT_TILE = SLAB
ATT_HEADS_PER_STEP = 8
ATT_Q_TILES_PER_STEP = 2
ATT_SCORE_BUFFERS = 4
ATT_SCORE_LOOKAHEAD = 2
ATT_ONES_ROWS = 16
LOG2_E = math.log2(math.e)
ROPE_PACK = V7X_LANES // (DA_DHEAD // 2)

F32 = jnp.float32
BF16 = jnp.bfloat16


def _dot(a, b):
    return jnp.dot(a, b, preferred_element_type=F32)


def _dot_nt(a, b):
    return lax.dot_general(a, b, (((1,), (1,)), ((), ())), preferred_element_type=F32)


def _dot_tn(a, b):
    return lax.dot_general(a, b, (((0,), (0,)), ((), ())), preferred_element_type=F32)


def _layer_norm(z, g, b):
    mu = jnp.mean(z, axis=-1, keepdims=True)
    zc = z - mu
    var = jnp.mean(zc * zc, axis=-1, keepdims=True)
    return zc * lax.rsqrt(var + LN_EPS) * g + b


def _whole_vmem():
    return pl.BlockSpec(memory_space=pltpu.VMEM)


def _params(n_axes):
    return pltpu.CompilerParams(dimension_semantics=("arbitrary",) * n_axes,
                                vmem_limit_bytes=V7X_VMEM_LIMIT)


_C_QK, _C_DQ, _C_DK, _C_GATE, _C_IF = (0, 1024, 2048, 3072, 5120)
_W_PACKED = _C_IF + V7X_LANES
_MVT_ROWS = ML_HEADS * (ML_DV + ML_ONES_ROWS)
_DVT_ROWS = DA_HEADS * (DA_DV + ATT_ONES_ROWS)
_GT_ROWS = 16
_R_MV, _R_OG, _R_DV, _R_GT = 0, _MVT_ROWS, _MVT_ROWS + 1024, _MVT_ROWS + 1024 + _DVT_ROWS
_WT_ROWS = _R_GT + _GT_ROWS


def _in_proj_kernel(x_ref, pos_ref, w_ref, b_ref, cw_ref, cb_ref, invf_ref, wt_ref, bt_ref,
                    qk_ref, dq_ref, dk_ref, gate_ref, gif_ref, mvt_ref, ogt_ref, dvt_ref, gt_ref,
                    pqk_ref, pdq_ref, pdk_ref, pg1_ref, pg2_ref, rt_ref, *, tiles_per_seq):
    tm = x_ref.shape[0]
    d = qk_ref.shape[1]
    half = d // 2

    halo = V7X_SUBLANES
    bqk = b_ref[:, _C_QK:_C_QK + d]

    @pl.when(pl.program_id(0) % tiles_per_seq == 0)
    def _():
        pqk_ref[0:halo, :] = jnp.broadcast_to(-bqk, (halo, d))

    taps = [cw_ref[j:j + 1, :] for j in range(ML_CONV)]
    conv_bias = cb_ref[...] + bqk * (taps[0] + taps[1] + taps[2] + taps[3])
    lane = lax.broadcasted_iota(jnp.int32, (1, V7X_LANES), 1)
    rope_w = V7X_LANES // ROPE_PACK

    n_sub = tm // SLAB

    def proj_to(dst_ref, xb, c0):
        dst_ref[...] = _dot(xb, w_ref[:, c0:c0 + d])

    def bias(c0, width):
        return b_ref[:, c0:c0 + width]

    xb = x_ref[0:SLAB, :].astype(BF16)
    proj_to(pqk_ref.at[halo:halo + SLAB], xb, _C_QK)
    for c in range(n_sub):
        rs = slice(c * SLAB, (c + 1) * SLAB)
        proj_to(pdq_ref, xb, _C_DQ)
        proj_to(pdk_ref, xb, _C_DK)
        rt_ref[...] = _dot_nt(wt_ref[...], xb)

        acc = conv_bias
        for j in range(ML_CONV):
            back = ML_CONV - 1 - j
            acc = acc + pqk_ref[halo - back:halo - back + SLAB, :] * taps[j]
        pqk_ref[0:halo, :] = pqk_ref[SLAB:SLAB + halo, :]
        y = acc * jax.nn.sigmoid(acc)
        qk_ref[rs, 0:half] = (y[:, 0:half] * (ML_DQK ** -0.5)).astype(BF16)
        qk_ref[rs, half:d] = y[:, half:d].astype(BF16)

        proj_to(pg1_ref, xb, _C_GATE)
        proj_to(pg2_ref, xb, _C_GATE + d)
        gif_ref[rs, :] = _dot(xb, w_ref[:, _C_IF:_C_IF + V7X_LANES]) + bias(_C_IF, V7X_LANES)

        pk = SLAB // ROPE_PACK
        ang = pos_ref[c * pk:(c + 1) * pk, :].astype(F32) * invf_ref[...]
        cos_p = jnp.cos(ang)
        sin_p = jnp.sin(ang)

        def spread(v):
            out = []
            for g in range(ROPE_PACK):
                u = v if g == 0 else pltpu.roll(v, V7X_LANES - g * rope_w, 1)
                u = jnp.where(lane < rope_w, u, pltpu.roll(u, rope_w, 1))
                out.append(jnp.where(lane < 2 * rope_w, u, pltpu.roll(u, 2 * rope_w, 1)))
            return jnp.concatenate(out, axis=0)

        cosv = spread(cos_p)
        sinv = spread(sin_p)
        ssin = jnp.where(lane < V7X_LANES // 2, -sinv, sinv)

        def rope_store(out_ref, p_ref, c0, scale):
            for h in range(d // V7X_LANES):
                cols = slice(h * V7X_LANES, (h + 1) * V7X_LANES)
                xh = p_ref[:, cols] + b_ref[:, c0 + h * V7X_LANES:c0 + (h + 1) * V7X_LANES]
                r = xh * cosv + pltpu.roll(xh, V7X_LANES // 2, 1) * ssin
                out_ref[rs, cols] = (r * scale).astype(BF16)

        rope_store(dq_ref, pdq_ref, _C_DQ, DA_DHEAD ** -0.5 * LOG2_E)
        rope_store(dk_ref, pdk_ref, _C_DK, 1.0)

        mvt_ref[c] = (rt_ref[_R_MV:_R_OG, :] + bt_ref[_R_MV:_R_OG, :]).astype(BF16)
        ogt_ref[c] = jax.nn.sigmoid(rt_ref[_R_OG:_R_DV, :] + bt_ref[_R_OG:_R_DV, :]).astype(BF16)
        dvt_ref[c] = (rt_ref[_R_DV:_R_GT, :] + bt_ref[_R_DV:_R_GT, :]).astype(BF16)
        gt_ref[c] = rt_ref[_R_GT:_WT_ROWS, :] + bt_ref[_R_GT:_WT_ROWS, :]

        if c + 1 < n_sub:
            xb = x_ref[(c + 1) * SLAB:(c + 2) * SLAB, :].astype(BF16)
            proj_to(pqk_ref.at[halo:halo + SLAB], xb, _C_QK)

        gate_ref[rs, 0:d] = jax.nn.sigmoid(pg1_ref[...] + bias(_C_GATE, d)).astype(BF16)
        gate_ref[rs, d:2 * d] = jax.nn.sigmoid(pg2_ref[...] + bias(_C_GATE + d, d)).astype(BF16)


def _in_proj(x2, pos2, w_packed, b_packed, conv_w, conv_b, invf, w_t, b_t, seq):
    t, d = x2.shape
    tm = ROW_TILE
    row_blk = lambda w: pl.BlockSpec((tm, w), lambda i: (i, 0))
    slab_blk = lambda r: pl.BlockSpec((tm // SLAB, r, SLAB), lambda i: (i, 0, 0))
    row_out = lambda w, dt: jax.ShapeDtypeStruct((t, w), dt)
    slab_out = lambda r, dt: jax.ShapeDtypeStruct((t // SLAB, r, SLAB), dt)
    return pl.pallas_call(
        functools.partial(_in_proj_kernel, tiles_per_seq=seq // tm),
        grid=(t // tm,),
        in_specs=[row_blk(d), pl.BlockSpec((tm // ROPE_PACK, V7X_LANES), lambda i: (i, 0))]
        + [_whole_vmem()] * 7,
        out_specs=[row_blk(d), row_blk(d), row_blk(d), row_blk(2 * d), row_blk(V7X_LANES),
                   slab_blk(_MVT_ROWS), slab_blk(d), slab_blk(_DVT_ROWS), slab_blk(_GT_ROWS)],
        out_shape=[row_out(d, BF16), row_out(d, BF16), row_out(d, BF16), row_out(2 * d, BF16),
                   row_out(V7X_LANES, F32), slab_out(_MVT_ROWS, BF16), slab_out(d, BF16),
                   slab_out(_DVT_ROWS, BF16), slab_out(_GT_ROWS, F32)],
        scratch_shapes=[pltpu.VMEM((V7X_SUBLANES + SLAB, d), F32)]
        + [pltpu.VMEM((SLAB, d), F32)] * 4 + [pltpu.VMEM((_WT_ROWS, SLAB), F32)],
        compiler_params=_params(1),
        name="in_proj",
    )(x2, pos2, w_packed, b_packed, conv_w, conv_b, invf, w_t, b_t)


def _split3_bf16(a):
    hi = a.astype(BF16)
    r1 = a - hi.astype(F32)
    mid = r1.astype(BF16)
    lo = (r1 - mid.astype(F32)).astype(BF16)
    return hi, mid, lo


def _mlstm_kernel(qk_ref, vt_ref, ogt_ref, gif_ref, gt_ref, nw_ref, out_ref,
                  c_ref, m_ref, qkt_ref, cq_ref):
    seq = qk_ref.shape[0]
    L = ML_CHUNK
    vrows = ML_DV + ML_ONES_ROWS
    c_ref[...] = jnp.zeros_like(c_ref)
    m_ref[...] = jnp.zeros_like(m_ref)

    si = lax.broadcasted_iota(jnp.int32, (L, L), 0)
    ti = lax.broadcasted_iota(jnp.int32, (L, L), 1)
    tri_lo = jnp.where(ti <= si, 1.0, 0.0).astype(BF16)
    tri_up = jnp.where(si <= ti, 1.0, 0.0).astype(BF16)
    causal_t = si <= ti
    nw = jnp.concatenate([nw_ref[...]] * (L // V7X_LANES), axis=1)

    def chunk(c, carry):
        rows = pl.ds(pl.multiple_of(c * L, L), L)
        for h in range(ML_HEADS):
            q = qk_ref[rows, h * ML_DQK:(h + 1) * ML_DQK]
            k = qk_ref[rows, (ML_HEADS + h) * ML_DQK:(ML_HEADS + h + 1) * ML_DQK]
            qkt_ref[h] = _dot_nt(k, q)
            cq_ref[h] = _dot_nt(c_ref[h].astype(BF16), q)
        gt = gt_ref[c]
        r3 = _split3_bf16(jax.nn.log_sigmoid(gt))
        bc_row = _dot(r3[0], tri_up) + _dot(r3[1], tri_up) + _dot(r3[2], tri_up)
        g = gif_ref[rows, :]
        c3 = _split3_bf16(jax.nn.log_sigmoid(g))
        bc_col = _dot(tri_lo, c3[0]) + _dot(tri_lo, c3[1]) + _dot(tri_lo, c3[2])
        z = g - pltpu.roll(bc_col, V7X_LANES - ML_HEADS, 1)
        for h in range(ML_HEADS):
            b_row = bc_row[ML_HEADS + h:ML_HEADS + h + 1, :]
            i_row = gt[h:h + 1, :]
            r_col = z[:, h:h + 1]
            m_prev = m_ref[h]
            dmat = jnp.where(causal_t, b_row + r_col, -jnp.inf)
            inter = b_row + m_prev
            m_row = jnp.maximum(inter, jnp.max(dmat, axis=0, keepdims=True))
            w_inter = jnp.exp(inter - m_row)
            pt = (qkt_ref[h] * jnp.exp(dmat - m_row)).astype(BF16)
            vt = vt_ref[c, h * vrows:(h + 1) * vrows, :]
            num = w_inter * cq_ref[h] + _dot(vt, pt)
            den = num[ML_DV:ML_DV + 1, :]
            rden = 1.0 / jnp.maximum(jnp.abs(den), jnp.exp(-m_row))
            b_last = b_row[:, L - 1:L]
            g_row = b_last - b_row + i_row
            m_new = jnp.maximum(b_last + m_prev, jnp.max(g_row, axis=1, keepdims=True))
            decay = jnp.exp(b_last + m_prev - m_new)
            ws = jnp.exp(g_row - m_new)
            k = qk_ref[rows, (ML_HEADS + h) * ML_DQK:(ML_HEADS + h + 1) * ML_DQK]
            c_ref[h] = decay * c_ref[h] + _dot((vt.astype(F32) * ws).astype(BF16), k)
            m_ref[h] = m_new
            xn = num[0:ML_DV, :]
            xc = xn - jnp.mean(xn, axis=0, keepdims=True)
            var = jnp.mean(xc * xc, axis=0, keepdims=True)
            scale = rden * lax.rsqrt(var * rden * rden + LN_EPS)
            hn = xc * scale * nw[h * ML_DV:(h + 1) * ML_DV, :]
            gate = ogt_ref[c, h * ML_DV:(h + 1) * ML_DV, :].astype(F32)
            out_ref[rows, h * ML_DV:(h + 1) * ML_DV] = (gate * hn).T.astype(BF16)
        return carry

    lax.fori_loop(0, seq // L, chunk, 0)


def _mlstm(qk, mvt, ogt, gif, gt, nw_lanes, batch, seq):
    t, d = qk.shape
    L = ML_CHUNK
    nc = seq // L
    vrows = ML_DV + ML_ONES_ROWS
    seq_blk = lambda w: pl.BlockSpec((seq, w), lambda b: (b, 0))
    slab_blk = lambda r: pl.BlockSpec((nc, r, L), lambda b: (b, 0, 0))
    return pl.pallas_call(
        _mlstm_kernel,
        grid=(batch,),
        in_specs=[seq_blk(d), slab_blk(_MVT_ROWS), slab_blk(d), seq_blk(V7X_LANES),
                  slab_blk(_GT_ROWS), _whole_vmem()],
        out_specs=seq_blk(d),
        out_shape=jax.ShapeDtypeStruct((t, d), BF16),
        scratch_shapes=[pltpu.VMEM((ML_HEADS, vrows, ML_DQK), F32),
                        pltpu.VMEM((ML_HEADS, 1, 1), F32),
                        pltpu.VMEM((ML_HEADS, L, L), F32),
                        pltpu.VMEM((ML_HEADS, vrows, L), F32)],
        compiler_params=_params(1),
        name="mlstm",
    )(qk, mvt, ogt, gif, gt, nw_lanes)


def _diff_attn_kernel(lq1_ref, lk1_ref, lq2_ref, lk2_ref, q_ref, k_ref, vt_ref, nw_ref,
                      o_ref, q12_ref, st_ref, mx_ref, m_ref, acc_ref, *, lam_init):
    tq = tk = ATT_TILE
    groups = q_ref.shape[0] // tq
    heads = q_ref.shape[1] // DA_DV
    n_slots = groups * heads
    i = pl.program_id(2)
    lam = (jnp.exp(jnp.sum(lq1_ref[...] * lk1_ref[...], axis=1, keepdims=True))
           - jnp.exp(jnp.sum(lq2_ref[...] * lk2_ref[...], axis=1, keepdims=True)) + lam_init)

    feat = lax.broadcasted_iota(jnp.int32, (DA_DV, tq), 0)
    first = (feat & (DA_DHEAD // 2)) == 0
    for s in range(n_slots):
        g, h = divmod(s, heads)
        qt = q_ref[g * tq:(g + 1) * tq, h * DA_DV:(h + 1) * DA_DV].astype(F32).T
        q12_ref[s, :, 0:tq] = jnp.where(first, qt, 0.0).astype(BF16)
        q12_ref[s, :, tq:2 * tq] = jnp.where(first, 0.0, qt).astype(BF16)

    vrows = DA_DV + ATT_ONES_ROWS
    n_st = st_ref.shape[0]
    ahead = ATT_SCORE_LOOKAHEAD
    assert ahead < n_st and n_slots % n_st == 0

    def scores(pos, j, s):
        h = s % heads
        rows = pl.ds(pl.multiple_of(j * tk, tk), tk)
        st = _dot(k_ref[rows, h * DA_DV:(h + 1) * DA_DV], q12_ref[s])
        st_ref[pos % n_st] = st
        mx_ref[pos % n_st] = jnp.max(st, axis=0, keepdims=True)

    def softmax(pos, s, mask_add):
        st = st_ref[pos % n_st]
        if mask_add is None:
            mx = mx_ref[pos % n_st]
        else:
            st = st + mask_add
            mx = jnp.max(st, axis=0, keepdims=True)
        m_prev = m_ref[s]
        m_new = jnp.maximum(m_prev, mx)
        alpha = jnp.exp2(m_prev - m_new)
        pt = jnp.exp2(st - m_new)
        m_ref[s] = m_new
        return alpha, pt.astype(BF16)

    def accumulate(j, s, alpha, pt):
        h = s % heads
        acc_ref[s] = alpha * acc_ref[s] + _dot(vt_ref[j, h * vrows:(h + 1) * vrows, :], pt)

    base = groups * i
    tail = [(u, s, s // heads == u) for u in range(groups)
            for s in range(n_slots) if s // heads >= u]

    for pos in range(ahead):
        scores(pos, 0, pos)
    m_ref[...] = jnp.full_like(m_ref, -jnp.inf)
    acc_ref[...] = jnp.zeros_like(acc_ref)

    def body(j, carry):
        for s in range(n_slots):
            alpha, pt = softmax(s, s, None)
            if s + ahead < n_slots:
                scores(s + ahead, j, s + ahead)
            else:
                scores(s + ahead, j + 1, s + ahead - n_slots)
            accumulate(j, s, alpha, pt)
        return carry

    lax.fori_loop(0, base, body, 0)
    kj = lax.broadcasted_iota(jnp.int32, (tk, 2 * tq), 0)
    qi = lax.broadcasted_iota(jnp.int32, (tk, 2 * tq), 1) & (tq - 1)
    causal_add = jnp.where(kj <= qi, 0.0, -jnp.inf)
    for pos, (u, s, diagonal) in enumerate(tail):
        alpha, pt = softmax(pos, s, causal_add if diagonal else None)
        if pos + ahead < len(tail):
            u2, s2, _ = tail[pos + ahead]
            scores(pos + ahead, base + u2, s2)
        accumulate(base + u, s, alpha, pt)

    for s in range(n_slots):
        g, h = divmod(s, heads)
        acc = acc_ref[s, 0:DA_DV, :]
        r = 1.0 / acc_ref[s, DA_DV:DA_DV + 1, :]
        ot = acc[:, 0:tq] * r[:, 0:tq] - acc[:, tq:2 * tq] * (lam * r[:, tq:2 * tq])
        ms = jnp.mean(ot * ot, axis=0, keepdims=True)
        o = (ot * (lax.rsqrt(ms + LN_EPS) * (1.0 - lam_init))).T
        o_ref[g * tq:(g + 1) * tq, h * DA_DV:(h + 1) * DA_DV] = (
            o * nw_ref[:, h * DA_DV:(h + 1) * DA_DV]).astype(BF16)


def _diff_attn(dq, dk, dvt, lam_vecs, da_norm_w, batch, seq, lam_init):
    t, d = dq.shape
    tq = ATT_TILE
    assert tq & (tq - 1) == 0
    nq = seq // tq
    hb = ATT_HEADS_PER_STEP
    qt = ATT_Q_TILES_PER_STEP
    assert nq % qt == 0
    steps = nq // qt
    slots = qt * hb
    w = hb * DA_DV
    vrows = DA_DV + ATT_ONES_ROWS
    lam_spec = pl.BlockSpec((1, DA_DHEAD), lambda b, g, i: (0, 0))
    q_spec = pl.BlockSpec((qt * tq, w), lambda b, g, i: (b * steps + i, g))
    return pl.pallas_call(
        functools.partial(_diff_attn_kernel, lam_init=lam_init),
        grid=(batch, DA_HEADS // hb, steps),
        in_specs=[lam_spec] * 4 + [q_spec,
                                   pl.BlockSpec((seq, w), lambda b, g, i: (b, g)),
                                   pl.BlockSpec((nq, hb * vrows, tq), lambda b, g, i: (b, g, 0)),
                                   pl.BlockSpec((1, w), lambda b, g, i: (0, g))],
        out_specs=q_spec,
        out_shape=jax.ShapeDtypeStruct((t, d), BF16),
        scratch_shapes=[pltpu.VMEM((slots, DA_DV, 2 * tq), BF16),
                        pltpu.VMEM((ATT_SCORE_BUFFERS, tq, 2 * tq), F32),
                        pltpu.VMEM((ATT_SCORE_BUFFERS, 1, 2 * tq), F32),
                        pltpu.VMEM((slots, 1, 2 * tq), F32),
                        pltpu.VMEM((slots, vrows, 2 * tq), F32)],
        compiler_params=_params(3),
        name="diff_attn",
    )(*lam_vecs, dq, dk, dvt, da_norm_w)


def _row_subs(tm):
    return [slice(r * ROW_SUB, (r + 1) * ROW_SUB) for r in range(tm // ROW_SUB)]


def _merge_kernel(x_ref, ha_ref, ob_ref, gate_ref, wa_ref, wb_ref, wm_ref, g_ref, b_ref, o_ref):
    d = x_ref.shape[1]
    subs = _row_subs(x_ref.shape[0])
    yab = [(_dot(ha_ref[s, :], wa_ref[...]), _dot(ob_ref[s, :], wb_ref[...])) for s in subs]
    mix = []
    for s, (ya, yb) in zip(subs, yab):
        mixin = gate_ref[s, 0:d].astype(F32) * ya + gate_ref[s, d:2 * d].astype(F32) * yb
        mix.append(_dot(mixin.astype(BF16), wm_ref[...]))
    for s, mx in zip(subs, mix):
        o_ref[s, :] = _layer_norm(ALPHA * x_ref[s, :] + mx, g_ref[...], b_ref[...])


def _merge(x2, ha, ob, gates, wa, wb, wm, g, b):
    t, d = x2.shape
    tm = ROW_TILE
    row_blk = lambda w: pl.BlockSpec((tm, w), lambda i: (i, 0))
    return pl.pallas_call(
        _merge_kernel,
        grid=(t // tm,),
        in_specs=[row_blk(d), row_blk(d), row_blk(d), row_blk(2 * d)] + [_whole_vmem()] * 5,
        out_specs=row_blk(d),
        out_shape=jax.ShapeDtypeStruct((t, d), F32),
        compiler_params=_params(1),
        name="merge_ln1",
    )(x2, ha, ob, gates, wa, wb, wm, g, b)


def _xattn_kv_kernel(mem_ref, wk_ref, wv_ref, k_ref, v_ref):
    mb = mem_ref[...].astype(BF16)
    k_ref[...] = _dot(mb, wk_ref[...]).astype(BF16)
    v_ref[...] = _dot(mb, wv_ref[...]).astype(BF16)


def _xattn_kv(mem2, wk, wv, mem_len):
    t, d = mem2.shape
    blk = pl.BlockSpec((mem_len, d), lambda b: (b, 0))
    return pl.pallas_call(
        _xattn_kv_kernel,
        grid=(t // mem_len,),
        in_specs=[blk, _whole_vmem(), _whole_vmem()],
        out_specs=[blk, blk],
        out_shape=[jax.ShapeDtypeStruct((t, d), BF16)] * 2,
        compiler_params=_params(1),
        name="xattn_kv",
    )(mem2, wk, wv)


def _xattn_kernel(x_ref, k_ref, v_ref, wq_ref, wo_ref, g_ref, b_ref, o_ref):
    d = x_ref.shape[1]
    dh = d // XA_HEADS
    subs = _row_subs(x_ref.shape[0])
    xq = [(_dot(x_ref[s, :].astype(BF16), wq_ref[...]) * (dh ** -0.5)).astype(BF16) for s in subs]
    xo = []
    for q in xq:
        heads = []
        for h in range(XA_HEADS):
            cols = slice(h * dh, (h + 1) * dh)
            sc = _dot_nt(q[:, cols], k_ref[:, cols])
            e = jnp.exp(sc - jnp.max(sc, axis=1, keepdims=True))
            p = e / jnp.sum(e, axis=1, keepdims=True)
            heads.append(_dot(p.astype(BF16), v_ref[:, cols]).astype(BF16))
        xo.append(_dot(jnp.concatenate(heads, axis=1), wo_ref[...]))
    for s, o in zip(subs, xo):
        o_ref[s, :] = _layer_norm(ALPHA * x_ref[s, :] + o, g_ref[...], b_ref[...])


def _xattn(x1, kx, vx, wq, wo, g, b, seq, mem_len):
    t, d = x1.shape
    tm = ROW_TILE
    tiles_per_seq = seq // tm
    row_blk = pl.BlockSpec((tm, d), lambda i: (i, 0))
    kv_blk = pl.BlockSpec((mem_len, d), lambda i: (i // tiles_per_seq, 0))
    return pl.pallas_call(
        _xattn_kernel,
        grid=(t // tm,),
        in_specs=[row_blk, kv_blk, kv_blk] + [_whole_vmem()] * 4,
        out_specs=row_blk,
        out_shape=jax.ShapeDtypeStruct((t, d), F32),
        compiler_params=_params(1),
        name="xattn_ln2",
    )(x1, kx, vx, wq, wo, g, b)


def _ffn_kernel(x_ref, wg_ref, wu_ref, wd_ref, g_ref, b_ref, o_ref):
    subs = _row_subs(x_ref.shape[0])
    gu = []
    for s in subs:
        xb = x_ref[s, :].astype(BF16)
        gu.append((_dot(xb, wg_ref[...]), _dot(xb, wu_ref[...])))
    down = [_dot((gt * jax.nn.sigmoid(gt) * up).astype(BF16), wd_ref[...]) for gt, up in gu]
    for s, dn in zip(subs, down):
        o_ref[s, :] = _layer_norm(ALPHA * x_ref[s, :] + dn, g_ref[...], b_ref[...])


def _ffn(x2, wg, wu, wd, g, b):
    t, d = x2.shape
    tm = ROW_TILE
    row_blk = pl.BlockSpec((tm, d), lambda i: (i, 0))
    return pl.pallas_call(
        _ffn_kernel,
        grid=(t // tm,),
        in_specs=[row_blk] + [_whole_vmem()] * 5,
        out_specs=row_blk,
        out_shape=jax.ShapeDtypeStruct((t, d), F32),
        compiler_params=_params(1),
        name="ffn_ln3",
    )(x2, wg, wu, wd, g, b)


def _rope_lane_order(a):
    lead = a.shape[:-1]
    a = a.reshape(lead + (DA_HEADS, 2, 2, DA_DHEAD // 2))
    return jnp.swapaxes(a, -3, -2).reshape(lead + (-1,))


def kernel(x, mem, positions, w_in, b_in, conv_w, conv_b, ml_norm_w, lam_q1, lam_k1, lam_q2, lam_k2, da_norm_w, w_proj_a, w_proj_b, w_mix_out, ln1_g, ln1_b, w_xq, w_xk, w_xv, w_xo, ln2_g, ln2_b, w_ffn_gate, w_ffn_up, w_ffn_down, ln3_g, ln3_b):
    batch, seq, d = x.shape
    mem_len = mem.shape[1]
    depth = w_in.shape[0]
    assert depth == DEPTH and seq % ROW_TILE == 0 and seq % ML_CHUNK == 0 and seq % ATT_TILE == 0
    t = batch * seq

    ml_qk_w = 2 * ML_HEADS * ML_DQK
    ml_v_w = ML_HEADS * ML_DV
    da_w = DA_HEADS * DA_DV
    sizes = (ml_qk_w, ml_v_w, ml_v_w, 2 * ML_HEADS, da_w, da_w, da_w, 2 * d)
    offs = np.concatenate([[0], np.cumsum(sizes)])
    assert ml_qk_w == d and ml_v_w == d and da_w == d and offs[-1] == w_in.shape[2]
    half = DA_DHEAD // 2
    invf = ROPE_THETA ** (-jnp.arange(half, dtype=F32) / half)
    invf = jnp.tile(invf, V7X_LANES // half).reshape(1, V7X_LANES)

    x2 = x.reshape(t, d)
    pos2 = jnp.repeat(positions.reshape(t // SLAB, ROPE_PACK, SLAB // ROPE_PACK).transpose(0, 2, 1),
                      V7X_LANES // ROPE_PACK, axis=2).reshape(t // ROPE_PACK, V7X_LANES)
    mem2 = mem.reshape(batch * mem_len, d)

    for l in range(depth):
        def piece(a, j):
            return a[..., offs[j]:offs[j + 1]]

        def pack(a):
            pad = jnp.zeros(a.shape[:-1] + (V7X_LANES - 2 * ML_HEADS,), a.dtype)
            return jnp.concatenate(
                [piece(a, 0), _rope_lane_order(piece(a, 4)), _rope_lane_order(piece(a, 5)),
                 piece(a, 7), piece(a, 3), pad], axis=-1)

        w_bf = w_in[l].astype(BF16)
        w_packed = pack(w_bf)
        b_packed = pack(b_in[l].reshape(1, -1))
        assert w_packed.shape[1] == _W_PACKED

        def with_ones(j, heads, dv, ones):
            wt = jnp.pad(piece(w_bf, j).T.reshape(heads, dv, d), ((0, 0), (0, ones), (0, 0)))
            bt = jnp.pad(piece(b_in[l], j).reshape(heads, dv), ((0, 0), (0, ones)),
                         constant_values=1.0)
            return wt.reshape(-1, d), bt.reshape(-1)

        w_mvt, b_mvt = with_ones(1, ML_HEADS, ML_DV, ML_ONES_ROWS)
        w_dvt, b_dvt = with_ones(6, DA_HEADS, DA_DV, ATT_ONES_ROWS)
        gpad = _GT_ROWS - 2 * ML_HEADS
        w_t = jnp.concatenate([w_mvt, piece(w_bf, 2).T, w_dvt,
                               jnp.pad(piece(w_bf, 3).T, ((0, gpad), (0, 0)))], axis=0)
        b_t = jnp.concatenate([b_mvt, piece(b_in[l], 2), b_dvt,
                               jnp.pad(piece(b_in[l], 3), (0, gpad))]).reshape(-1, 1)
        assert w_t.shape[0] == _WT_ROWS

        qk, dq, dk, gates, gif, mvt, ogt, dvt, gt = _in_proj(
            x2, pos2, w_packed, b_packed, conv_w[l], conv_b[l].reshape(1, -1), invf,
            w_t, b_t, seq)

        nw_lanes = jnp.broadcast_to(ml_norm_w[l].reshape(-1, 1), (ML_HEADS * ML_DV, V7X_LANES))
        ha = _mlstm(qk, mvt, ogt, gif, gt, nw_lanes, batch, seq)

        lam_init = 0.8 - 0.6 * math.exp(-0.3 * l)
        lam_vecs = [a[l].reshape(1, -1) for a in (lam_q1, lam_k1, lam_q2, lam_k2)]
        ob = _diff_attn(dq, dk, dvt, lam_vecs, da_norm_w[l].reshape(1, -1), batch, seq, lam_init)

        x2 = _merge(x2, ha, ob, gates, w_proj_a[l].astype(BF16), w_proj_b[l].astype(BF16),
                    w_mix_out[l].astype(BF16), ln1_g[l].reshape(1, -1), ln1_b[l].reshape(1, -1))

        kx, vx = _xattn_kv(mem2, w_xk[l].astype(BF16), w_xv[l].astype(BF16), mem_len)
        x2 = _xattn(x2, kx, vx, w_xq[l].astype(BF16), w_xo[l].astype(BF16),
                    ln2_g[l].reshape(1, -1), ln2_b[l].reshape(1, -1), seq, mem_len)

        x2 = _ffn(x2, w_ffn_gate[l].astype(BF16), w_ffn_up[l].astype(BF16),
                  w_ffn_down[l].astype(BF16), ln3_g[l].reshape(1, -1), ln3_b[l].reshape(1, -1))

    return x2.reshape(batch, seq, d)
```

```python
import functools
import math

import numpy as np
import jax
import jax.numpy as jnp
from jax import lax
from jax.experimental import pallas as pl
from jax.experimental.pallas import tpu as pltpu

ML_HEADS = 4
ML_DQK = 128
ML_DV = 256
ML_CONV = 4
DA_HEADS = 8
DA_DHEAD = 64
DA_DV = 2 * DA_DHEAD
XA_HEADS = 4
ROPE_THETA = 10000.0
DEPTH = 1
ALPHA = (2.0 * DEPTH) ** 0.25
LN_EPS = 1e-5

V7X_LANES = 128
V7X_SUBLANES = 8
V7X_VMEM_LIMIT = 56 * 1024 * 1024

ROW_TILE = 512
ROW_SUB = 256
SLAB = 256
ML_CHUNK = SLAB
ML_ONES_ROWS = 16
ATT_TILE = SLAB
ATT_HEADS_PER_STEP = 8
ATT_Q_TILES_PER_STEP = 2
ATT_SCORE_BUFFERS = 4
ATT_SCORE_LOOKAHEAD = 2
ATT_ONES_ROWS = 16
LOG2_E = math.log2(math.e)
ROPE_PACK = V7X_LANES // (DA_DHEAD // 2)

F32 = jnp.float32
BF16 = jnp.bfloat16


def _dot(a, b):
    return jnp.dot(a, b, preferred_element_type=F32)


def _dot_nt(a, b):
    return lax.dot_general(a, b, (((1,), (1,)), ((), ())), preferred_element_type=F32)


def _dot_tn(a, b):
    return lax.dot_general(a, b, (((0,), (0,)), ((), ())), preferred_element_type=F32)


def _layer_norm(z, g, b):
    mu = jnp.mean(z, axis=-1, keepdims=True)
    zc = z - mu
    var = jnp.mean(zc * zc, axis=-1, keepdims=True)
    return zc * lax.rsqrt(var + LN_EPS) * g + b


def _whole_vmem():
    return pl.BlockSpec(memory_space=pltpu.VMEM)


def _params(n_axes):
    return pltpu.CompilerParams(dimension_semantics=("arbitrary",) * n_axes,
                                vmem_limit_bytes=V7X_VMEM_LIMIT)


_C_QK, _C_DQ, _C_DK, _C_GATE, _C_IF = (0, 1024, 2048, 3072, 5120)
_W_PACKED = _C_IF + V7X_LANES
_MVT_ROWS = ML_HEADS * (ML_DV + ML_ONES_ROWS)
_DVT_ROWS = DA_HEADS * (DA_DV + ATT_ONES_ROWS)
_GT_ROWS = 16
_R_MV, _R_OG, _R_DV, _R_GT = 0, _MVT_ROWS, _MVT_ROWS + 1024, _MVT_ROWS + 1024 + _DVT_ROWS
_WT_ROWS = _R_GT + _GT_ROWS


def _in_proj_kernel(x_ref, pos_ref, w_ref, b_ref, cw_ref, cb_ref, invf_ref, wt_ref, bt_ref,
                    qk_ref, dq_ref, dk_ref, gate_ref, gif_ref, mvt_ref, ogt_ref, dvt_ref, gt_ref,
                    pqk_ref, pdq_ref, pdk_ref, pg1_ref, pg2_ref, rt_ref, *, tiles_per_seq):
    tm = x_ref.shape[0]
    d = qk_ref.shape[1]
    half = d // 2

    halo = V7X_SUBLANES
    bqk = b_ref[:, _C_QK:_C_QK + d]

    @pl.when(pl.program_id(0) % tiles_per_seq == 0)
    def _():
        pqk_ref[0:halo, :] = jnp.broadcast_to(-bqk, (halo, d))

    taps = [cw_ref[j:j + 1, :] for j in range(ML_CONV)]
    conv_bias = cb_ref[...] + bqk * (taps[0] + taps[1] + taps[2] + taps[3])
    lane = lax.broadcasted_iota(jnp.int32, (1, V7X_LANES), 1)
    rope_w = V7X_LANES // ROPE_PACK

    n_sub = tm // SLAB

    def proj_to(dst_ref, xb, c0):
        dst_ref[...] = _dot(xb, w_ref[:, c0:c0 + d])

    def bias(c0, width):
        return b_ref[:, c0:c0 + width]

    xb = x_ref[0:SLAB, :].astype(BF16)
    proj_to(pqk_ref.at[halo:halo + SLAB], xb, _C_QK)
    for c in range(n_sub):
        rs = slice(c * SLAB, (c + 1) * SLAB)
        proj_to(pdq_ref, xb, _C_DQ)
        proj_to(pdk_ref, xb, _C_DK)
        rt_ref[...] = _dot_nt(wt_ref[...], xb)

        acc = conv_bias
        for j in range(ML_CONV):
            back = ML_CONV - 1 - j
            acc = acc + pqk_ref[halo - back:halo - back + SLAB, :] * taps[j]
        pqk_ref[0:halo, :] = pqk_ref[SLAB:SLAB + halo, :]
        y = acc * jax.nn.sigmoid(acc)
        qk_ref[rs, 0:half] = (y[:, 0:half] * (ML_DQK ** -0.5)).astype(BF16)
        qk_ref[rs, half:d] = y[:, half:d].astype(BF16)

        proj_to(pg1_ref, xb, _C_GATE)
        proj_to(pg2_ref, xb, _C_GATE + d)
        gif_ref[rs, :] = _dot(xb, w_ref[:, _C_IF:_C_IF + V7X_LANES]) + bias(_C_IF, V7X_LANES)

        pk = SLAB // ROPE_PACK
        ang = pos_ref[c * pk:(c + 1) * pk, :].astype(F32) * invf_ref[...]
        cos_p = jnp.cos(ang)
        sin_p = jnp.sin(ang)

        def spread(v):
            out = []
            for g in range(ROPE_PACK):
                u = v if g == 0 else pltpu.roll(v, V7X_LANES - g * rope_w, 1)
                u = jnp.where(lane < rope_w, u, pltpu.roll(u, rope_w, 1))
                out.append(jnp.where(lane < 2 * rope_w, u, pltpu.roll(u, 2 * rope_w, 1)))
            return jnp.concatenate(out, axis=0)

        cosv = spread(cos_p)
        sinv = spread(sin_p)
        ssin = jnp.where(lane < V7X_LANES // 2, -sinv, sinv)

        def rope_store(out_ref, p_ref, c0, scale):
            for h in range(d // V7X_LANES):
                cols = slice(h * V7X_LANES, (h + 1) * V7X_LANES)
                xh = p_ref[:, cols] + b_ref[:, c0 + h * V7X_LANES:c0 + (h + 1) * V7X_LANES]
                r = xh * cosv + pltpu.roll(xh, V7X_LANES // 2, 1) * ssin
                out_ref[rs, cols] = (r * scale).astype(BF16)

        rope_store(dq_ref, pdq_ref, _C_DQ, DA_DHEAD ** -0.5 * LOG2_E)
        rope_store(dk_ref, pdk_ref, _C_DK, 1.0)

        mvt_ref[c] = (rt_ref[_R_MV:_R_OG, :] + bt_ref[_R_MV:_R_OG, :]).astype(BF16)
        ogt_ref[c] = jax.nn.sigmoid(rt_ref[_R_OG:_R_DV, :] + bt_ref[_R_OG:_R_DV, :]).astype(BF16)
        dvt_ref[c] = (rt_ref[_R_DV:_R_GT, :] + bt_ref[_R_DV:_R_GT, :]).astype(BF16)
        gt_ref[c] = rt_ref[_R_GT:_WT_ROWS, :] + bt_ref[_R_GT:_WT_ROWS, :]

        if c + 1 < n_sub:
            xb = x_ref[(c + 1) * SLAB:(c + 2) * SLAB, :].astype(BF16)
            proj_to(pqk_ref.at[halo:halo + SLAB], xb, _C_QK)

        gate_ref[rs, 0:d] = jax.nn.sigmoid(pg1_ref[...] + bias(_C_GATE, d)).astype(BF16)
        gate_ref[rs, d:2 * d] = jax.nn.sigmoid(pg2_ref[...] + bias(_C_GATE + d, d)).astype(BF16)


def _in_proj(x2, pos2, w_packed, b_packed, conv_w, conv_b, invf, w_t, b_t, seq):
    t, d = x2.shape
    tm = ROW_TILE
    row_blk = lambda w: pl.BlockSpec((tm, w), lambda i: (i, 0))
    slab_blk = lambda r: pl.BlockSpec((tm // SLAB, r, SLAB), lambda i: (i, 0, 0))
    row_out = lambda w, dt: jax.ShapeDtypeStruct((t, w), dt)
    slab_out = lambda r, dt: jax.ShapeDtypeStruct((t // SLAB, r, SLAB), dt)
    return pl.pallas_call(
        functools.partial(_in_proj_kernel, tiles_per_seq=seq // tm),
        grid=(t // tm,),
        in_specs=[row_blk(d), pl.BlockSpec((tm // ROPE_PACK, V7X_LANES), lambda i: (i, 0))]
        + [_whole_vmem()] * 7,
        out_specs=[row_blk(d), row_blk(d), row_blk(d), row_blk(2 * d), row_blk(V7X_LANES),
                   slab_blk(_MVT_ROWS), slab_blk(d), slab_blk(_DVT_ROWS), slab_blk(_GT_ROWS)],
        out_shape=[row_out(d, BF16), row_out(d, BF16), row_out(d, BF16), row_out(2 * d, BF16),
                   row_out(V7X_LANES, F32), slab_out(_MVT_ROWS, BF16), slab_out(d, BF16),
                   slab_out(_DVT_ROWS, BF16), slab_out(_GT_ROWS, F32)],
        scratch_shapes=[pltpu.VMEM((V7X_SUBLANES + SLAB, d), F32)]
        + [pltpu.VMEM((SLAB, d), F32)] * 4 + [pltpu.VMEM((_WT_ROWS, SLAB), F32)],
        compiler_params=_params(1),
        name="in_proj",
    )(x2, pos2, w_packed, b_packed, conv_w, conv_b, invf, w_t, b_t)


def _split3_bf16(a):
    hi = a.astype(BF16)
    r1 = a - hi.astype(F32)
    mid = r1.astype(BF16)
    lo = (r1 - mid.astype(F32)).astype(BF16)
    return hi, mid, lo


def _mlstm_kernel(qk_ref, vt_ref, ogt_ref, gif_ref, gt_ref, nw_ref, out_ref,
                  c_ref, m_ref, qkt_ref, cq_ref, bcr_ref, z_ref):
    seq = qk_ref.shape[0]
    L = ML_CHUNK
    vrows = ML_DV + ML_ONES_ROWS
    c_ref[...] = jnp.zeros_like(c_ref)
    m_ref[...] = jnp.zeros_like(m_ref)

    si = lax.broadcasted_iota(jnp.int32, (L, L), 0)
    ti = lax.broadcasted_iota(jnp.int32, (L, L), 1)
    tri_lo = jnp.where(ti <= si, 1.0, 0.0).astype(BF16)
    tri_up = jnp.where(si <= ti, 1.0, 0.0).astype(BF16)
    causal_t = si <= ti
    nw = jnp.concatenate([nw_ref[...]] * (L // V7X_LANES), axis=1)

    n_chunks = seq // L

    def rows_of(c):
        return pl.ds(pl.multiple_of(c * L, L), L)

    def gate_sums(c):
        r3 = _split3_bf16(jax.nn.log_sigmoid(gt_ref[c]))
        bcr_ref[c % 2] = _dot(r3[0], tri_up) + _dot(r3[1], tri_up) + _dot(r3[2], tri_up)
        g = gif_ref[rows_of(c), :]
        c3 = _split3_bf16(jax.nn.log_sigmoid(g))
        bc_col = _dot(tri_lo, c3[0]) + _dot(tri_lo, c3[1]) + _dot(tri_lo, c3[2])
        z_ref[c % 2] = g - pltpu.roll(bc_col, V7X_LANES - ML_HEADS, 1)

    def key_query(c, h):
        q = qk_ref[rows_of(c), h * ML_DQK:(h + 1) * ML_DQK]
        k = qk_ref[rows_of(c), (ML_HEADS + h) * ML_DQK:(ML_HEADS + h + 1) * ML_DQK]
        qkt_ref[h] = _dot_nt(k, q)

    gate_sums(0)
    for h in range(ML_HEADS):
        key_query(0, h)

    def chunk(c, carry):
        rows = rows_of(c)
        nxt = jnp.minimum(c + 1, n_chunks - 1)
        gt = gt_ref[c]
        bc_row = bcr_ref[c % 2]
        z = z_ref[c % 2]
        for h in range(ML_HEADS):
            q = qk_ref[rows, h * ML_DQK:(h + 1) * ML_DQK]
            cq_ref[h] = _dot_nt(c_ref[h].astype(BF16), q)
        gate_sums(nxt)
        for h in range(ML_HEADS):
            b_row = bc_row[ML_HEADS + h:ML_HEADS + h + 1, :]
            i_row = gt[h:h + 1, :]
            r_col = z[:, h:h + 1]
            m_prev = m_ref[h]
            dmat = jnp.where(causal_t, b_row + r_col, -jnp.inf)
            inter = b_row + m_prev
            m_row = jnp.maximum(inter, jnp.max(dmat, axis=0, keepdims=True))
            w_inter = jnp.exp(inter - m_row)
            pt = (qkt_ref[h] * jnp.exp(dmat - m_row)).astype(BF16)
            key_query(nxt, h)
            vt = vt_ref[c, h * vrows:(h + 1) * vrows, :]
            num = w_inter * cq_ref[h] + _dot(vt, pt)
            den = num[ML_DV:ML_DV + 1, :]
            rden = 1.0 / jnp.maximum(jnp.abs(den), jnp.exp(-m_row))
            b_last = b_row[:, L - 1:L]
            g_row = b_last - b_row + i_row
            m_new = jnp.maximum(b_last + m_prev, jnp.max(g_row, axis=1, keepdims=True))
            decay = jnp.exp(b_last + m_prev - m_new)
            ws = jnp.exp(g_row - m_new)
            k = qk_ref[rows, (ML_HEADS + h) * ML_DQK:(ML_HEADS + h + 1) * ML_DQK]
            c_ref[h] = decay * c_ref[h] + _dot((vt.astype(F32) * ws).astype(BF16), k)
            m_ref[h] = m_new
            xn = num[0:ML_DV, :]
            xc = xn - jnp.mean(xn, axis=0, keepdims=True)
            var = jnp.mean(xc * xc, axis=0, keepdims=True)
            scale = rden * lax.rsqrt(var * rden * rden + LN_EPS)
            hn = xc * scale * nw[h * ML_DV:(h + 1) * ML_DV, :]
            gate = ogt_ref[c, h * ML_DV:(h + 1) * ML_DV, :].astype(F32)
            out_ref[rows, h * ML_DV:(h + 1) * ML_DV] = (gate * hn).T.astype(BF16)
        return carry

    lax.fori_loop(0, seq // L, chunk, 0)


def _mlstm(qk, mvt, ogt, gif, gt, nw_lanes, batch, seq):
    t, d = qk.shape
    L = ML_CHUNK
    nc = seq // L
    vrows = ML_DV + ML_ONES_ROWS
    seq_blk = lambda w: pl.BlockSpec((seq, w), lambda b: (b, 0))
    slab_blk = lambda r: pl.BlockSpec((nc, r, L), lambda b: (b, 0, 0))
    return pl.pallas_call(
        _mlstm_kernel,
        grid=(batch,),
        in_specs=[seq_blk(d), slab_blk(_MVT_ROWS), slab_blk(d), seq_blk(V7X_LANES),
                  slab_blk(_GT_ROWS), _whole_vmem()],
        out_specs=seq_blk(d),
        out_shape=jax.ShapeDtypeStruct((t, d), BF16),
        scratch_shapes=[pltpu.VMEM((ML_HEADS, vrows, ML_DQK), F32),
                        pltpu.VMEM((ML_HEADS, 1, 1), F32),
                        pltpu.VMEM((ML_HEADS, L, L), F32),
                        pltpu.VMEM((ML_HEADS, vrows, L), F32),
                        pltpu.VMEM((2, _GT_ROWS, L), F32),
                        pltpu.VMEM((2, L, V7X_LANES), F32)],
        compiler_params=_params(1),
        name="mlstm",
    )(qk, mvt, ogt, gif, gt, nw_lanes)


def _diff_attn_kernel(lq1_ref, lk1_ref, lq2_ref, lk2_ref, q_ref, k_ref, vt_ref, nw_ref,
                      o_ref, q12_ref, st_ref, mx_ref, m_ref, acc_ref, *, lam_init):
    tq = tk = ATT_TILE
    groups = q_ref.shape[0] // tq
    heads = q_ref.shape[1] // DA_DV
    n_slots = groups * heads
    i = pl.program_id(2)
    lam = (jnp.exp(jnp.sum(lq1_ref[...] * lk1_ref[...], axis=1, keepdims=True))
           - jnp.exp(jnp.sum(lq2_ref[...] * lk2_ref[...], axis=1, keepdims=True)) + lam_init)

    feat = lax.broadcasted_iota(jnp.int32, (DA_DV, tq), 0)
    first = (feat & (DA_DHEAD // 2)) == 0
    for s in range(n_slots):
        g, h = divmod(s, heads)
        qt = q_ref[g * tq:(g + 1) * tq, h * DA_DV:(h + 1) * DA_DV].astype(F32).T
        q12_ref[s, :, 0:tq] = jnp.where(first, qt, 0.0).astype(BF16)
        q12_ref[s, :, tq:2 * tq] = jnp.where(first, 0.0, qt).astype(BF16)

    vrows = DA_DV + ATT_ONES_ROWS
    n_st = st_ref.shape[0]
    ahead = ATT_SCORE_LOOKAHEAD
    assert ahead < n_st and n_slots % n_st == 0

    def scores(pos, j, s):
        h = s % heads
        rows = pl.ds(pl.multiple_of(j * tk, tk), tk)
        st = _dot(k_ref[rows, h * DA_DV:(h + 1) * DA_DV], q12_ref[s])
        st_ref[pos % n_st] = st
        mx_ref[pos % n_st] = jnp.max(st, axis=0, keepdims=True)

    def softmax(pos, s, mask_add):
        st = st_ref[pos % n_st]
        if mask_add is None:
            mx = mx_ref[pos % n_st]
        else:
            st = st + mask_add
            mx = jnp.max(st, axis=0, keepdims=True)
        m_prev = m_ref[s]
        m_new = jnp.maximum(m_prev, mx)
        alpha = jnp.exp2(m_prev - m_new)
        pt = jnp.exp2(st - m_new)
        m_ref[s] = m_new
        return alpha, pt.astype(BF16)

    def accumulate(j, s, alpha, pt):
        h = s % heads
        acc_ref[s] = alpha * acc_ref[s] + _dot(vt_ref[j, h * vrows:(h + 1) * vrows, :], pt)

    base = groups * i
    tail = [(u, s, s // heads == u) for u in range(groups)
            for s in range(n_slots) if s // heads >= u]

    for pos in range(ahead):
        scores(pos, 0, pos)
    m_ref[...] = jnp.full_like(m_ref, -jnp.inf)
    acc_ref[...] = jnp.zeros_like(acc_ref)

    def body(j, carry):
        for s in range(n_slots):
            alpha, pt = softmax(s, s, None)
            if s + ahead < n_slots:
                scores(s + ahead, j, s + ahead)
            else:
                scores(s + ahead, j + 1, s + ahead - n_slots)
            accumulate(j, s, alpha, pt)
        return carry

    lax.fori_loop(0, base, body, 0)
    kj = lax.broadcasted_iota(jnp.int32, (tk, 2 * tq), 0)
    qi = lax.broadcasted_iota(jnp.int32, (tk, 2 * tq), 1) & (tq - 1)
    causal_add = jnp.where(kj <= qi, 0.0, -jnp.inf)
    for pos, (u, s, diagonal) in enumerate(tail):
        alpha, pt = softmax(pos, s, causal_add if diagonal else None)
        if pos + ahead < len(tail):
            u2, s2, _ = tail[pos + ahead]
            scores(pos + ahead, base + u2, s2)
        accumulate(base + u, s, alpha, pt)

    for s in range(n_slots):
        g, h = divmod(s, heads)
        acc = acc_ref[s, 0:DA_DV, :]
        r = 1.0 / acc_ref[s, DA_DV:DA_DV + 1, :]
        ot = acc[:, 0:tq] * r[:, 0:tq] - acc[:, tq:2 * tq] * (lam * r[:, tq:2 * tq])
        ms = jnp.mean(ot * ot, axis=0, keepdims=True)
        o = (ot * (lax.rsqrt(ms + LN_EPS) * (1.0 - lam_init))).T
        o_ref[g * tq:(g + 1) * tq, h * DA_DV:(h + 1) * DA_DV] = (
            o * nw_ref[:, h * DA_DV:(h + 1) * DA_DV]).astype(BF16)


def _diff_attn(dq, dk, dvt, lam_vecs, da_norm_w, batch, seq, lam_init):
    t, d = dq.shape
    tq = ATT_TILE
    assert tq & (tq - 1) == 0
    nq = seq // tq
    hb = ATT_HEADS_PER_STEP
    qt = ATT_Q_TILES_PER_STEP
    assert nq % qt == 0
    steps = nq // qt
    slots = qt * hb
    w = hb * DA_DV
    vrows = DA_DV + ATT_ONES_ROWS
    lam_spec = pl.BlockSpec((1, DA_DHEAD), lambda b, g, i: (0, 0))
    q_spec = pl.BlockSpec((qt * tq, w), lambda b, g, i: (b * steps + i, g))
    return pl.pallas_call(
        functools.partial(_diff_attn_kernel, lam_init=lam_init),
        grid=(batch, DA_HEADS // hb, steps),
        in_specs=[lam_spec] * 4 + [q_spec,
                                   pl.BlockSpec((seq, w), lambda b, g, i: (b, g)),
                                   pl.BlockSpec((nq, hb * vrows, tq), lambda b, g, i: (b, g, 0)),
                                   pl.BlockSpec((1, w), lambda b, g, i: (0, g))],
        out_specs=q_spec,
        out_shape=jax.ShapeDtypeStruct((t, d), BF16),
        scratch_shapes=[pltpu.VMEM((slots, DA_DV, 2 * tq), BF16),
                        pltpu.VMEM((ATT_SCORE_BUFFERS, tq, 2 * tq), F32),
                        pltpu.VMEM((ATT_SCORE_BUFFERS, 1, 2 * tq), F32),
                        pltpu.VMEM((slots, 1, 2 * tq), F32),
                        pltpu.VMEM((slots, vrows, 2 * tq), F32)],
        compiler_params=_params(3),
        name="diff_attn",
    )(*lam_vecs, dq, dk, dvt, da_norm_w)


def _row_subs(tm):
    return [slice(r * ROW_SUB, (r + 1) * ROW_SUB) for r in range(tm // ROW_SUB)]


def _merge_kernel(x_ref, ha_ref, ob_ref, gate_ref, wa_ref, wb_ref, wm_ref, g_ref, b_ref, o_ref):
    d = x_ref.shape[1]
    subs = _row_subs(x_ref.shape[0])
    yab = [(_dot(ha_ref[s, :], wa_ref[...]), _dot(ob_ref[s, :], wb_ref[...])) for s in subs]
    mix = []
    for s, (ya, yb) in zip(subs, yab):
        mixin = gate_ref[s, 0:d].astype(F32) * ya + gate_ref[s, d:2 * d].astype(F32) * yb
        mix.append(_dot(mixin.astype(BF16), wm_ref[...]))
    for s, mx in zip(subs, mix):
        o_ref[s, :] = _layer_norm(ALPHA * x_ref[s, :] + mx, g_ref[...], b_ref[...])


def _merge(x2, ha, ob, gates, wa, wb, wm, g, b):
    t, d = x2.shape
    tm = ROW_TILE
    row_blk = lambda w: pl.BlockSpec((tm, w), lambda i: (i, 0))
    return pl.pallas_call(
        _merge_kernel,
        grid=(t // tm,),
        in_specs=[row_blk(d), row_blk(d), row_blk(d), row_blk(2 * d)] + [_whole_vmem()] * 5,
        out_specs=row_blk(d),
        out_shape=jax.ShapeDtypeStruct((t, d), F32),
        compiler_params=_params(1),
        name="merge_ln1",
    )(x2, ha, ob, gates, wa, wb, wm, g, b)


def _xattn_kv_kernel(mem_ref, wk_ref, wv_ref, k_ref, v_ref):
    mb = mem_ref[...].astype(BF16)
    k_ref[...] = _dot(mb, wk_ref[...]).astype(BF16)
    v_ref[...] = _dot(mb, wv_ref[...]).astype(BF16)


def _xattn_kv(mem2, wk, wv, mem_len):
    t, d = mem2.shape
    blk = pl.BlockSpec((mem_len, d), lambda b: (b, 0))
    return pl.pallas_call(
        _xattn_kv_kernel,
        grid=(t // mem_len,),
        in_specs=[blk, _whole_vmem(), _whole_vmem()],
        out_specs=[blk, blk],
        out_shape=[jax.ShapeDtypeStruct((t, d), BF16)] * 2,
        compiler_params=_params(1),
        name="xattn_kv",
    )(mem2, wk, wv)


def _xattn_kernel(x_ref, k_ref, v_ref, wq_ref, wo_ref, g_ref, b_ref, o_ref):
    d = x_ref.shape[1]
    dh = d // XA_HEADS
    subs = _row_subs(x_ref.shape[0])
    xq = [(_dot(x_ref[s, :].astype(BF16), wq_ref[...]) * (dh ** -0.5)).astype(BF16) for s in subs]
    xo = []
    for q in xq:
        heads = []
        for h in range(XA_HEADS):
            cols = slice(h * dh, (h + 1) * dh)
            sc = _dot_nt(q[:, cols], k_ref[:, cols])
            e = jnp.exp(sc - jnp.max(sc, axis=1, keepdims=True))
            p = e / jnp.sum(e, axis=1, keepdims=True)
            heads.append(_dot(p.astype(BF16), v_ref[:, cols]).astype(BF16))
        xo.append(_dot(jnp.concatenate(heads, axis=1), wo_ref[...]))
    for s, o in zip(subs, xo):
        o_ref[s, :] = _layer_norm(ALPHA * x_ref[s, :] + o, g_ref[...], b_ref[...])


def _xattn(x1, kx, vx, wq, wo, g, b, seq, mem_len):
    t, d = x1.shape
    tm = ROW_TILE
    tiles_per_seq = seq // tm
    row_blk = pl.BlockSpec((tm, d), lambda i: (i, 0))
    kv_blk = pl.BlockSpec((mem_len, d), lambda i: (i // tiles_per_seq, 0))
    return pl.pallas_call(
        _xattn_kernel,
        grid=(t // tm,),
        in_specs=[row_blk, kv_blk, kv_blk] + [_whole_vmem()] * 4,
        out_specs=row_blk,
        out_shape=jax.ShapeDtypeStruct((t, d), F32),
        compiler_params=_params(1),
        name="xattn_ln2",
    )(x1, kx, vx, wq, wo, g, b)


def _ffn_kernel(x_ref, wg_ref, wu_ref, wd_ref, g_ref, b_ref, o_ref):
    subs = _row_subs(x_ref.shape[0])
    gu = []
    for s in subs:
        xb = x_ref[s, :].astype(BF16)
        gu.append((_dot(xb, wg_ref[...]), _dot(xb, wu_ref[...])))
    down = [_dot((gt * jax.nn.sigmoid(gt) * up).astype(BF16), wd_ref[...]) for gt, up in gu]
    for s, dn in zip(subs, down):
        o_ref[s, :] = _layer_norm(ALPHA * x_ref[s, :] + dn, g_ref[...], b_ref[...])


def _ffn(x2, wg, wu, wd, g, b):
    t, d = x2.shape
    tm = ROW_TILE
    row_blk = pl.BlockSpec((tm, d), lambda i: (i, 0))
    return pl.pallas_call(
        _ffn_kernel,
        grid=(t // tm,),
        in_specs=[row_blk] + [_whole_vmem()] * 5,
        out_specs=row_blk,
        out_shape=jax.ShapeDtypeStruct((t, d), F32),
        compiler_params=_params(1),
        name="ffn_ln3",
    )(x2, wg, wu, wd, g, b)


def _rope_lane_order(a):
    lead = a.shape[:-1]
    a = a.reshape(lead + (DA_HEADS, 2, 2, DA_DHEAD // 2))
    return jnp.swapaxes(a, -3, -2).reshape(lead + (-1,))


def kernel(x, mem, positions, w_in, b_in, conv_w, conv_b, ml_norm_w, lam_q1, lam_k1, lam_q2, lam_k2, da_norm_w, w_proj_a, w_proj_b, w_mix_out, ln1_g, ln1_b, w_xq, w_xk, w_xv, w_xo, ln2_g, ln2_b, w_ffn_gate, w_ffn_up, w_ffn_down, ln3_g, ln3_b):
    batch, seq, d = x.shape
    mem_len = mem.shape[1]
    depth = w_in.shape[0]
    assert depth == DEPTH and seq % ROW_TILE == 0 and seq % ML_CHUNK == 0 and seq % ATT_TILE == 0
    t = batch * seq

    ml_qk_w = 2 * ML_HEADS * ML_DQK
    ml_v_w = ML_HEADS * ML_DV
    da_w = DA_HEADS * DA_DV
    sizes = (ml_qk_w, ml_v_w, ml_v_w, 2 * ML_HEADS, da_w, da_w, da_w, 2 * d)
    offs = np.concatenate([[0], np.cumsum(sizes)])
    assert ml_qk_w == d and ml_v_w == d and da_w == d and offs[-1] == w_in.shape[2]
    half = DA_DHEAD // 2
    invf = ROPE_THETA ** (-jnp.arange(half, dtype=F32) / half)
    invf = jnp.tile(invf, V7X_LANES // half).reshape(1, V7X_LANES)

    x2 = x.reshape(t, d)
    pos2 = jnp.repeat(positions.reshape(t // SLAB, ROPE_PACK, SLAB // ROPE_PACK).transpose(0, 2, 1),
                      V7X_LANES // ROPE_PACK, axis=2).reshape(t // ROPE_PACK, V7X_LANES)
    mem2 = mem.reshape(batch * mem_len, d)

    for l in range(depth):
        def piece(a, j):
            return a[..., offs[j]:offs[j + 1]]

        def pack(a):
            pad = jnp.zeros(a.shape[:-1] + (V7X_LANES - 2 * ML_HEADS,), a.dtype)
            return jnp.concatenate(
                [piece(a, 0), _rope_lane_order(piece(a, 4)), _rope_lane_order(piece(a, 5)),
                 piece(a, 7), piece(a, 3), pad], axis=-1)

        w_bf = w_in[l].astype(BF16)
        w_packed = pack(w_bf)
        b_packed = pack(b_in[l].reshape(1, -1))
        assert w_packed.shape[1] == _W_PACKED

        def with_ones(j, heads, dv, ones):
            wt = jnp.pad(piece(w_bf, j).T.reshape(heads, dv, d), ((0, 0), (0, ones), (0, 0)))
            bt = jnp.pad(piece(b_in[l], j).reshape(heads, dv), ((0, 0), (0, ones)),
                         constant_values=1.0)
            return wt.reshape(-1, d), bt.reshape(-1)

        w_mvt, b_mvt = with_ones(1, ML_HEADS, ML_DV, ML_ONES_ROWS)
        w_dvt, b_dvt = with_ones(6, DA_HEADS, DA_DV, ATT_ONES_ROWS)
        gpad = _GT_ROWS - 2 * ML_HEADS
        w_t = jnp.concatenate([w_mvt, piece(w_bf, 2).T, w_dvt,
                               jnp.pad(piece(w_bf, 3).T, ((0, gpad), (0, 0)))], axis=0)
        b_t = jnp.concatenate([b_mvt, piece(b_in[l], 2), b_dvt,
                               jnp.pad(piece(b_in[l], 3), (0, gpad))]).reshape(-1, 1)
        assert w_t.shape[0] == _WT_ROWS

        qk, dq, dk, gates, gif, mvt, ogt, dvt, gt = _in_proj(
            x2, pos2, w_packed, b_packed, conv_w[l], conv_b[l].reshape(1, -1), invf,
            w_t, b_t, seq)

        nw_lanes = jnp.broadcast_to(ml_norm_w[l].reshape(-1, 1), (ML_HEADS * ML_DV, V7X_LANES))
        ha = _mlstm(qk, mvt, ogt, gif, gt, nw_lanes, batch, seq)

        lam_init = 0.8 - 0.6 * math.exp(-0.3 * l)
        lam_vecs = [a[l].reshape(1, -1) for a in (lam_q1, lam_k1, lam_q2, lam_k2)]
        ob = _diff_attn(dq, dk, dvt, lam_vecs, da_norm_w[l].reshape(1, -1), batch, seq, lam_init)

        x2 = _merge(x2, ha, ob, gates, w_proj_a[l].astype(BF16), w_proj_b[l].astype(BF16),
                    w_mix_out[l].astype(BF16), ln1_g[l].reshape(1, -1), ln1_b[l].reshape(1, -1))

        kx, vx = _xattn_kv(mem2, w_xk[l].astype(BF16), w_xv[l].astype(BF16), mem_len)
        x2 = _xattn(x2, kx, vx, w_xq[l].astype(BF16), w_xo[l].astype(BF16),
                    ln2_g[l].reshape(1, -1), ln2_b[l].reshape(1, -1), seq, mem_len)

        x2 = _ffn(x2, w_ffn_gate[l].astype(BF16), w_ffn_up[l].astype(BF16),
                  w_ffn_down[l].astype(BF16), ln3_g[l].reshape(1, -1), ln3_b[l].reshape(1, -1))

    return x2.reshape(batch, seq, d)
```

```python
import functools
import math

import numpy as np
import jax
import jax.numpy as jnp
from jax import lax
from jax.experimental import pallas as pl
from jax.experimental.pallas import tpu as pltpu

ML_HEADS = 4
ML_DQK = 128
ML_DV = 256
ML_CONV = 4
DA_HEADS = 8
DA_DHEAD = 64
DA_DV = 2 * DA_DHEAD
XA_HEADS = 4
ROPE_THETA = 10000.0
DEPTH = 1
ALPHA = (2.0 * DEPTH) ** 0.25
LN_EPS = 1e-5

V7X_LANES = 128
V7X_SUBLANES = 8
V7X_VMEM_LIMIT = 56 * 1024 * 1024

ROW_TILE = 512
ROW_SUB = 256
SLAB = 256
ML_CHUNK = SLAB
ML_ONES_ROWS = 16
ATT_TILE = SLAB
ATT_HEADS_PER_STEP = 8
ATT_Q_TILES_PER_STEP = 2
ATT_SCORE_BUFFERS = 4
ATT_SCORE_LOOKAHEAD = 2
ATT_ONES_ROWS = 16
LOG2_E = math.log2(math.e)
ROPE_PACK = V7X_LANES // (DA_DHEAD // 2)

F32 = jnp.float32
BF16 = jnp.bfloat16


def _dot(a, b):
    return jnp.dot(a, b, preferred_element_type=F32)


def _dot_nt(a, b):
    return lax.dot_general(a, b, (((1,), (1,)), ((), ())), preferred_element_type=F32)


def _dot_tn(a, b):
    return lax.dot_general(a, b, (((0,), (0,)), ((), ())), preferred_element_type=F32)


def _layer_norm(z, g, b):
    mu = jnp.mean(z, axis=-1, keepdims=True)
    zc = z - mu
    var = jnp.mean(zc * zc, axis=-1, keepdims=True)
    return zc * lax.rsqrt(var + LN_EPS) * g + b


def _whole_vmem():
    return pl.BlockSpec(memory_space=pltpu.VMEM)


def _params(n_axes):
    return pltpu.CompilerParams(dimension_semantics=("arbitrary",) * n_axes,
                                vmem_limit_bytes=V7X_VMEM_LIMIT)


_C_QK, _C_DQ, _C_DK, _C_GATE, _C_IF = (0, 1024, 2048, 3072, 5120)
_W_PACKED = _C_IF + V7X_LANES
_MVT_ROWS = ML_HEADS * (ML_DV + ML_ONES_ROWS)
_DVT_ROWS = DA_HEADS * (DA_DV + ATT_ONES_ROWS)
_GT_ROWS = 16
_R_MV, _R_OG, _R_DV, _R_GT = 0, _MVT_ROWS, _MVT_ROWS + 1024, _MVT_ROWS + 1024 + _DVT_ROWS
_WT_ROWS = _R_GT + _GT_ROWS


def _in_proj_kernel(x_ref, pos_ref, w_ref, b_ref, cw_ref, cb_ref, invf_ref, wt_ref, bt_ref,
                    qk_ref, dq_ref, dk_ref, gate_ref, gif_ref, mvt_ref, ogt_ref, dvt_ref, gt_ref,
                    pqk_ref, pdq_ref, pdk_ref, pg1_ref, pg2_ref, rt_ref, *, tiles_per_seq):
    tm = x_ref.shape[0]
    d = qk_ref.shape[1]
    half = d // 2

    halo = V7X_SUBLANES
    bqk = b_ref[:, _C_QK:_C_QK + d]

    @pl.when(pl.program_id(0) % tiles_per_seq == 0)
    def _():
        pqk_ref[0:halo, :] = jnp.broadcast_to(-bqk, (halo, d))

    taps = [cw_ref[j:j + 1, :] for j in range(ML_CONV)]
    conv_bias = cb_ref[...] + bqk * (taps[0] + taps[1] + taps[2] + taps[3])
    lane = lax.broadcasted_iota(jnp.int32, (1, V7X_LANES), 1)
    rope_w = V7X_LANES // ROPE_PACK

    n_sub = tm // SLAB

    def proj_to(dst_ref, xb, c0):
        dst_ref[...] = _dot(xb, w_ref[:, c0:c0 + d])

    def bias(c0, width):
        return b_ref[:, c0:c0 + width]

    xb = x_ref[0:SLAB, :].astype(BF16)
    proj_to(pqk_ref.at[halo:halo + SLAB], xb, _C_QK)
    for c in range(n_sub):
        rs = slice(c * SLAB, (c + 1) * SLAB)
        proj_to(pdq_ref, xb, _C_DQ)
        proj_to(pdk_ref, xb, _C_DK)
        rt_ref[...] = _dot_nt(wt_ref[...], xb)

        acc = conv_bias
        for j in range(ML_CONV):
            back = ML_CONV - 1 - j
            acc = acc + pqk_ref[halo - back:halo - back + SLAB, :] * taps[j]
        pqk_ref[0:halo, :] = pqk_ref[SLAB:SLAB + halo, :]
        y = acc * jax.nn.sigmoid(acc)
        qk_ref[rs, 0:half] = (y[:, 0:half] * (ML_DQK ** -0.5)).astype(BF16)
        qk_ref[rs, half:d] = y[:, half:d].astype(BF16)

        proj_to(pg1_ref, xb, _C_GATE)
        proj_to(pg2_ref, xb, _C_GATE + d)
        gif_ref[rs, :] = _dot(xb, w_ref[:, _C_IF:_C_IF + V7X_LANES]) + bias(_C_IF, V7X_LANES)

        pk = SLAB // ROPE_PACK
        ang = pos_ref[c * pk:(c + 1) * pk, :].astype(F32) * invf_ref[...]
        cos_p = jnp.cos(ang)
        sin_p = jnp.sin(ang)

        def spread(v):
            out = []
            for g in range(ROPE_PACK):
                u = v if g == 0 else pltpu.roll(v, V7X_LANES - g * rope_w, 1)
                u = jnp.where(lane < rope_w, u, pltpu.roll(u, rope_w, 1))
                out.append(jnp.where(lane < 2 * rope_w, u, pltpu.roll(u, 2 * rope_w, 1)))
            return jnp.concatenate(out, axis=0)

        cosv = spread(cos_p)
        sinv = spread(sin_p)
        ssin = jnp.where(lane < V7X_LANES // 2, -sinv, sinv)

        def rope_store(out_ref, p_ref, c0, scale):
            for h in range(d // V7X_LANES):
                cols = slice(h * V7X_LANES, (h + 1) * V7X_LANES)
                xh = p_ref[:, cols] + b_ref[:, c0 + h * V7X_LANES:c0 + (h + 1) * V7X_LANES]
                r = xh * cosv + pltpu.roll(xh, V7X_LANES // 2, 1) * ssin
                out_ref[rs, cols] = (r * scale).astype(BF16)

        rope_store(dq_ref, pdq_ref, _C_DQ, DA_DHEAD ** -0.5 * LOG2_E)
        rope_store(dk_ref, pdk_ref, _C_DK, 1.0)

        mvt_ref[c] = (rt_ref[_R_MV:_R_OG, :] + bt_ref[_R_MV:_R_OG, :]).astype(BF16)
        ogt_ref[c] = jax.nn.sigmoid(rt_ref[_R_OG:_R_DV, :] + bt_ref[_R_OG:_R_DV, :]).astype(BF16)
        dvt_ref[c] = (rt_ref[_R_DV:_R_GT, :] + bt_ref[_R_DV:_R_GT, :]).astype(BF16)
        gt_ref[c] = rt_ref[_R_GT:_WT_ROWS, :] + bt_ref[_R_GT:_WT_ROWS, :]

        if c + 1 < n_sub:
            xb = x_ref[(c + 1) * SLAB:(c + 2) * SLAB, :].astype(BF16)
            proj_to(pqk_ref.at[halo:halo + SLAB], xb, _C_QK)

        gate_ref[rs, 0:d] = jax.nn.sigmoid(pg1_ref[...] + bias(_C_GATE, d)).astype(BF16)
        gate_ref[rs, d:2 * d] = jax.nn.sigmoid(pg2_ref[...] + bias(_C_GATE + d, d)).astype(BF16)


def _in_proj(x2, pos2, w_packed, b_packed, conv_w, conv_b, invf, w_t, b_t, seq):
    t, d = x2.shape
    tm = ROW_TILE
    row_blk = lambda w: pl.BlockSpec((tm, w), lambda i: (i, 0))
    slab_blk = lambda r: pl.BlockSpec((tm // SLAB, r, SLAB), lambda i: (i, 0, 0))
    row_out = lambda w, dt: jax.ShapeDtypeStruct((t, w), dt)
    slab_out = lambda r, dt: jax.ShapeDtypeStruct((t // SLAB, r, SLAB), dt)
    return pl.pallas_call(
        functools.partial(_in_proj_kernel, tiles_per_seq=seq // tm),
        grid=(t // tm,),
        in_specs=[row_blk(d), pl.BlockSpec((tm // ROPE_PACK, V7X_LANES), lambda i: (i, 0))]
        + [_whole_vmem()] * 7,
        out_specs=[row_blk(d), row_blk(d), row_blk(d), row_blk(2 * d), row_blk(V7X_LANES),
                   slab_blk(_MVT_ROWS), slab_blk(d), slab_blk(_DVT_ROWS), slab_blk(_GT_ROWS)],
        out_shape=[row_out(d, BF16), row_out(d, BF16), row_out(d, BF16), row_out(2 * d, BF16),
                   row_out(V7X_LANES, F32), slab_out(_MVT_ROWS, BF16), slab_out(d, BF16),
                   slab_out(_DVT_ROWS, BF16), slab_out(_GT_ROWS, F32)],
        scratch_shapes=[pltpu.VMEM((V7X_SUBLANES + SLAB, d), F32)]
        + [pltpu.VMEM((SLAB, d), F32)] * 4 + [pltpu.VMEM((_WT_ROWS, SLAB), F32)],
        compiler_params=_params(1),
        name="in_proj",
    )(x2, pos2, w_packed, b_packed, conv_w, conv_b, invf, w_t, b_t)


def _split3_bf16(a):
    hi = a.astype(BF16)
    r1 = a - hi.astype(F32)
    mid = r1.astype(BF16)
    lo = (r1 - mid.astype(F32)).astype(BF16)
    return hi, mid, lo


def _mlstm_kernel(qk_ref, vt_ref, ogt_ref, gif_ref, gt_ref, nw_ref, out_ref,
                  c_ref, m_ref, qkt_ref, cq_ref, bcr_ref, z_ref):
    seq = qk_ref.shape[0]
    L = ML_CHUNK
    vrows = ML_DV + ML_ONES_ROWS
    c_ref[...] = jnp.zeros_like(c_ref)
    m_ref[...] = jnp.zeros_like(m_ref)

    si = lax.broadcasted_iota(jnp.int32, (L, L), 0)
    ti = lax.broadcasted_iota(jnp.int32, (L, L), 1)
    tri_lo = jnp.where(ti <= si, 1.0, 0.0).astype(BF16)
    tri_up = jnp.where(si <= ti, 1.0, 0.0).astype(BF16)
    causal_t = si <= ti
    nw = jnp.concatenate([nw_ref[...]] * (L // V7X_LANES), axis=1)

    n_chunks = seq // L

    def rows_of(c):
        return pl.ds(pl.multiple_of(c * L, L), L)

    def gate_sums(c):
        r3 = _split3_bf16(jax.nn.log_sigmoid(gt_ref[c]))
        bcr_ref[c % 2] = _dot(r3[0], tri_up) + _dot(r3[1], tri_up) + _dot(r3[2], tri_up)
        g = gif_ref[rows_of(c), :]
        c3 = _split3_bf16(jax.nn.log_sigmoid(g))
        bc_col = _dot(tri_lo, c3[0]) + _dot(tri_lo, c3[1]) + _dot(tri_lo, c3[2])
        z_ref[c % 2] = g - pltpu.roll(bc_col, V7X_LANES - ML_HEADS, 1)

    def key_query(c, h):
        q = qk_ref[rows_of(c), h * ML_DQK:(h + 1) * ML_DQK]
        k = qk_ref[rows_of(c), (ML_HEADS + h) * ML_DQK:(ML_HEADS + h + 1) * ML_DQK]
        qkt_ref[h] = _dot_nt(k, q)

    gate_sums(0)
    for h in range(ML_HEADS):
        key_query(0, h)

    def chunk(c, carry):
        rows = rows_of(c)
        nxt = jnp.minimum(c + 1, n_chunks - 1)
        gt = gt_ref[c]
        bc_row = bcr_ref[c % 2]
        z = z_ref[c % 2]
        for h in range(ML_HEADS):
            q = qk_ref[rows, h * ML_DQK:(h + 1) * ML_DQK]
            cq_ref[h] = _dot_nt(c_ref[h].astype(BF16), q)
        gate_sums(nxt)
        for h in range(ML_HEADS):
            b_row = bc_row[ML_HEADS + h:ML_HEADS + h + 1, :]
            i_row = gt[h:h + 1, :]
            r_col = z[:, h:h + 1]
            m_prev = m_ref[h]
            dmat = jnp.where(causal_t, b_row + r_col, -jnp.inf)
            inter = b_row + m_prev
            m_row = jnp.maximum(inter, jnp.max(dmat, axis=0, keepdims=True))
            w_inter = jnp.exp(inter - m_row)
            pt = (qkt_ref[h] * jnp.exp(dmat - m_row)).astype(BF16)
            key_query(nxt, h)
            vt = vt_ref[c, h * vrows:(h + 1) * vrows, :]
            num = w_inter * cq_ref[h] + _dot(vt, pt)
            den = num[ML_DV:ML_DV + 1, :]
            rden = 1.0 / jnp.maximum(jnp.abs(den), jnp.exp(-m_row))
            b_last = b_row[:, L - 1:L]
            g_row = b_last - b_row + i_row
            m_new = jnp.maximum(b_last + m_prev, jnp.max(g_row, axis=1, keepdims=True))
            decay = jnp.exp(b_last + m_prev - m_new)
            ws = jnp.exp(g_row - m_new)
            k = qk_ref[rows, (ML_HEADS + h) * ML_DQK:(ML_HEADS + h + 1) * ML_DQK]
            c_ref[h] = decay * c_ref[h] + _dot((vt.astype(F32) * ws).astype(BF16), k)
            m_ref[h] = m_new
            xn = num[0:ML_DV, :]
            xc = xn - jnp.mean(xn, axis=0, keepdims=True)
            var = jnp.mean(xc * xc, axis=0, keepdims=True)
            scale = rden * lax.rsqrt(var * rden * rden + LN_EPS)
            hn = xc * scale * nw[h * ML_DV:(h + 1) * ML_DV, :]
            gate = ogt_ref[c, h * ML_DV:(h + 1) * ML_DV, :].astype(F32)
            out_ref[rows, h * ML_DV:(h + 1) * ML_DV] = (gate * hn).T.astype(BF16)
        return carry

    lax.fori_loop(0, seq // L, chunk, 0)


def _mlstm(qk, mvt, ogt, gif, gt, nw_lanes, batch, seq):
    t, d = qk.shape
    L = ML_CHUNK
    nc = seq // L
    vrows = ML_DV + ML_ONES_ROWS
    seq_blk = lambda w: pl.BlockSpec((seq, w), lambda b: (b, 0))
    slab_blk = lambda r: pl.BlockSpec((nc, r, L), lambda b: (b, 0, 0))
    return pl.pallas_call(
        _mlstm_kernel,
        grid=(batch,),
        in_specs=[seq_blk(d), slab_blk(_MVT_ROWS), slab_blk(d), seq_blk(V7X_LANES),
                  slab_blk(_GT_ROWS), _whole_vmem()],
        out_specs=seq_blk(d),
        out_shape=jax.ShapeDtypeStruct((t, d), BF16),
        scratch_shapes=[pltpu.VMEM((ML_HEADS, vrows, ML_DQK), F32),
                        pltpu.VMEM((ML_HEADS, 1, 1), F32),
                        pltpu.VMEM((ML_HEADS, L, L), F32),
                        pltpu.VMEM((ML_HEADS, vrows, L), F32),
                        pltpu.VMEM((2, _GT_ROWS, L), F32),
                        pltpu.VMEM((2, L, V7X_LANES), F32)],
        compiler_params=_params(1),
        name="mlstm",
    )(qk, mvt, ogt, gif, gt, nw_lanes)


def _diff_attn_kernel(lq1_ref, lk1_ref, lq2_ref, lk2_ref, q_ref, k_ref, vt_ref, nw_ref,
                      o_ref, q12_ref, st_ref, mx_ref, m_ref, acc_ref, *, lam_init):
    tq = tk = ATT_TILE
    groups = q_ref.shape[0] // tq
    heads = q_ref.shape[1] // DA_DV
    n_slots = groups * heads
    i = pl.program_id(2)
    lam = (jnp.exp(jnp.sum(lq1_ref[...] * lk1_ref[...], axis=1, keepdims=True))
           - jnp.exp(jnp.sum(lq2_ref[...] * lk2_ref[...], axis=1, keepdims=True)) + lam_init)

    feat = lax.broadcasted_iota(jnp.int32, (DA_DV, tq), 0)
    first = (feat & (DA_DHEAD // 2)) == 0
    for s in range(n_slots):
        g, h = divmod(s, heads)
        qt = q_ref[g * tq:(g + 1) * tq, h * DA_DV:(h + 1) * DA_DV].astype(F32).T
        q12_ref[s, :, 0:tq] = jnp.where(first, qt, 0.0).astype(BF16)
        q12_ref[s, :, tq:2 * tq] = jnp.where(first, 0.0, qt).astype(BF16)

    vrows = DA_DV + ATT_ONES_ROWS
    n_st = st_ref.shape[0]
    ahead = ATT_SCORE_LOOKAHEAD
    assert ahead < n_st and n_slots % n_st == 0

    def scores(pos, j, s):
        h = s % heads
        rows = pl.ds(pl.multiple_of(j * tk, tk), tk)
        st = _dot(k_ref[rows, h * DA_DV:(h + 1) * DA_DV], q12_ref[s])
        st_ref[pos % n_st] = st
        mx_ref[pos % n_st] = jnp.max(st, axis=0, keepdims=True)

    def softmax(pos, s, mask_add):
        st = st_ref[pos % n_st]
        if mask_add is None:
            mx = mx_ref[pos % n_st]
        else:
            st = st + mask_add
            mx = jnp.max(st, axis=0, keepdims=True)
        m_prev = m_ref[s]
        m_new = jnp.maximum(m_prev, mx)
        alpha = jnp.exp2(m_prev - m_new)
        pt = jnp.exp2(st - m_new)
        m_ref[s] = m_new
        return alpha, pt.astype(BF16)

    def accumulate(j, s, alpha, pt):
        h = s % heads
        acc_ref[s] = alpha * acc_ref[s] + _dot(vt_ref[j, h * vrows:(h + 1) * vrows, :], pt)

    base = groups * i
    tail = [(u, s, s // heads == u) for u in range(groups)
            for s in range(n_slots) if s // heads >= u]

    for pos in range(ahead):
        scores(pos, 0, pos)
    m_ref[...] = jnp.full_like(m_ref, -jnp.inf)
    acc_ref[...] = jnp.zeros_like(acc_ref)

    def body(j, carry):
        for s in range(n_slots):
            alpha, pt = softmax(s, s, None)
            if s + ahead < n_slots:
                scores(s + ahead, j, s + ahead)
            else:
                scores(s + ahead, j + 1, s + ahead - n_slots)
            accumulate(j, s, alpha, pt)
        return carry

    lax.fori_loop(0, base, body, 0)
    kj = lax.broadcasted_iota(jnp.int32, (tk, 2 * tq), 0)
    qi = lax.broadcasted_iota(jnp.int32, (tk, 2 * tq), 1) & (tq - 1)
    causal_add = jnp.where(kj <= qi, 0.0, -jnp.inf)
    for pos, (u, s, diagonal) in enumerate(tail):
        alpha, pt = softmax(pos, s, causal_add if diagonal else None)
        if pos + ahead < len(tail):
            u2, s2, _ = tail[pos + ahead]
            scores(pos + ahead, base + u2, s2)
        accumulate(base + u, s, alpha, pt)

    for s in range(n_slots):
        g, h = divmod(s, heads)
        acc = acc_ref[s, 0:DA_DV, :]
        r = 1.0 / acc_ref[s, DA_DV:DA_DV + 1, :]
        ot = acc[:, 0:tq] * r[:, 0:tq] - acc[:, tq:2 * tq] * (lam * r[:, tq:2 * tq])
        ms = jnp.mean(ot * ot, axis=0, keepdims=True)
        o = (ot * (lax.rsqrt(ms + LN_EPS) * (1.0 - lam_init))).T
        o_ref[g * tq:(g + 1) * tq, h * DA_DV:(h + 1) * DA_DV] = (
            o * nw_ref[:, h * DA_DV:(h + 1) * DA_DV]).astype(BF16)


def _diff_attn(dq, dk, dvt, lam_vecs, da_norm_w, batch, seq, lam_init):
    t, d = dq.shape
    tq = ATT_TILE
    assert tq & (tq - 1) == 0
    nq = seq // tq
    hb = ATT_HEADS_PER_STEP
    qt = ATT_Q_TILES_PER_STEP
    assert nq % qt == 0
    steps = nq // qt
    slots = qt * hb
    w = hb * DA_DV
    vrows = DA_DV + ATT_ONES_ROWS
    lam_spec = pl.BlockSpec((1, DA_DHEAD), lambda b, g, i: (0, 0))
    q_spec = pl.BlockSpec((qt * tq, w), lambda b, g, i: (b * steps + i, g))
    return pl.pallas_call(
        functools.partial(_diff_attn_kernel, lam_init=lam_init),
        grid=(batch, DA_HEADS // hb, steps),
        in_specs=[lam_spec] * 4 + [q_spec,
                                   pl.BlockSpec((seq, w), lambda b, g, i: (b, g)),
                                   pl.BlockSpec((nq, hb * vrows, tq), lambda b, g, i: (b, g, 0)),
                                   pl.BlockSpec((1, w), lambda b, g, i: (0, g))],
        out_specs=q_spec,
        out_shape=jax.ShapeDtypeStruct((t, d), BF16),
        scratch_shapes=[pltpu.VMEM((slots, DA_DV, 2 * tq), BF16),
                        pltpu.VMEM((ATT_SCORE_BUFFERS, tq, 2 * tq), F32),
                        pltpu.VMEM((ATT_SCORE_BUFFERS, 1, 2 * tq), F32),
                        pltpu.VMEM((slots, 1, 2 * tq), F32),
                        pltpu.VMEM((slots, vrows, 2 * tq), F32)],
        compiler_params=_params(3),
        name="diff_attn",
    )(*lam_vecs, dq, dk, dvt, da_norm_w)


def _row_subs(tm):
    return [slice(r * ROW_SUB, (r + 1) * ROW_SUB) for r in range(tm // ROW_SUB)]


def _merge_kernel(x_ref, ha_ref, ob_ref, gate_ref, wa_ref, wb_ref, wm_ref, g_ref, b_ref, o_ref):
    d = x_ref.shape[1]
    subs = _row_subs(x_ref.shape[0])
    yab = [(_dot(ha_ref[s, :], wa_ref[...]), _dot(ob_ref[s, :], wb_ref[...])) for s in subs]
    mix = []
    for s, (ya, yb) in zip(subs, yab):
        mixin = gate_ref[s, 0:d].astype(F32) * ya + gate_ref[s, d:2 * d].astype(F32) * yb
        mix.append(_dot(mixin.astype(BF16), wm_ref[...]))
    for s, mx in zip(subs, mix):
        o_ref[s, :] = _layer_norm(ALPHA * x_ref[s, :] + mx, g_ref[...], b_ref[...])


def _merge(x2, ha, ob, gates, wa, wb, wm, g, b):
    t, d = x2.shape
    tm = ROW_TILE
    row_blk = lambda w: pl.BlockSpec((tm, w), lambda i: (i, 0))
    return pl.pallas_call(
        _merge_kernel,
        grid=(t // tm,),
        in_specs=[row_blk(d), row_blk(d), row_blk(d), row_blk(2 * d)] + [_whole_vmem()] * 5,
        out_specs=row_blk(d),
        out_shape=jax.ShapeDtypeStruct((t, d), F32),
        compiler_params=_params(1),
        name="merge_ln1",
    )(x2, ha, ob, gates, wa, wb, wm, g, b)


def _xattn_kv_kernel(mem_ref, wk_ref, wv_ref, k_ref, v_ref):
    mb = mem_ref[...].astype(BF16)
    k_ref[...] = _dot(mb, wk_ref[...]).astype(BF16)
    v_ref[...] = _dot(mb, wv_ref[...]).astype(BF16)


def _xattn_kv(mem2, wk, wv, mem_len):
    t, d = mem2.shape
    blk = pl.BlockSpec((mem_len, d), lambda b: (b, 0))
    return pl.pallas_call(
        _xattn_kv_kernel,
        grid=(t // mem_len,),
        in_specs=[blk, _whole_vmem(), _whole_vmem()],
        out_specs=[blk, blk],
        out_shape=[jax.ShapeDtypeStruct((t, d), BF16)] * 2,
        compiler_params=_params(1),
        name="xattn_kv",
    )(mem2, wk, wv)


def _xattn_kernel(x_ref, k_ref, v_ref, wq_ref, wo_ref, g_ref, b_ref, o_ref):
    d = x_ref.shape[1]
    dh = d // XA_HEADS
    subs = _row_subs(x_ref.shape[0])
    head_cols = [slice(h * dh, (h + 1) * dh) for h in range(XA_HEADS)]

    def project_q(s):
        return (_dot(x_ref[s, :].astype(BF16), wq_ref[...]) * (dh ** -0.5)).astype(BF16)

    def scores(q):
        return [_dot_nt(q[:, c], k_ref[:, c]) for c in head_cols]

    def softmax(sc):
        e = jnp.exp(sc - jnp.max(sc, axis=1, keepdims=True))
        return (e / jnp.sum(e, axis=1, keepdims=True)).astype(BF16)

    def finish(r, xo):
        o_ref[subs[r], :] = _layer_norm(ALPHA * x_ref[subs[r], :] + xo, g_ref[...], b_ref[...])

    n = len(subs)
    q = project_q(subs[0])
    sc = scores(q)
    xo_prev = None
    for r in range(n):
        if r + 1 < n:
            q = project_q(subs[r + 1])
        ps = [softmax(s_h) for s_h in sc]
        o = jnp.concatenate([_dot(p, v_ref[:, c]).astype(BF16) for p, c in zip(ps, head_cols)],
                            axis=1)
        if r + 1 < n:
            sc = scores(q)
        xo = _dot(o, wo_ref[...])
        if xo_prev is not None:
            finish(r - 1, xo_prev)
        xo_prev = xo
    finish(n - 1, xo_prev)


def _xattn(x1, kx, vx, wq, wo, g, b, seq, mem_len):
    t, d = x1.shape
    tm = ROW_TILE
    tiles_per_seq = seq // tm
    row_blk = pl.BlockSpec((tm, d), lambda i: (i, 0))
    kv_blk = pl.BlockSpec((mem_len, d), lambda i: (i // tiles_per_seq, 0))
    return pl.pallas_call(
        _xattn_kernel,
        grid=(t // tm,),
        in_specs=[row_blk, kv_blk, kv_blk] + [_whole_vmem()] * 4,
        out_specs=row_blk,
        out_shape=jax.ShapeDtypeStruct((t, d), F32),
        compiler_params=_params(1),
        name="xattn_ln2",
    )(x1, kx, vx, wq, wo, g, b)


def _ffn_kernel(x_ref, wg_ref, wu_ref, wd_ref, g_ref, b_ref, o_ref):
    subs = _row_subs(x_ref.shape[0])
    gu = []
    for s in subs:
        xb = x_ref[s, :].astype(BF16)
        gu.append((_dot(xb, wg_ref[...]), _dot(xb, wu_ref[...])))
    down = [_dot((gt * jax.nn.sigmoid(gt) * up).astype(BF16), wd_ref[...]) for gt, up in gu]
    for s, dn in zip(subs, down):
        o_ref[s, :] = _layer_norm(ALPHA * x_ref[s, :] + dn, g_ref[...], b_ref[...])


def _ffn(x2, wg, wu, wd, g, b):
    t, d = x2.shape
    tm = ROW_TILE
    row_blk = pl.BlockSpec((tm, d), lambda i: (i, 0))
    return pl.pallas_call(
        _ffn_kernel,
        grid=(t // tm,),
        in_specs=[row_blk] + [_whole_vmem()] * 5,
        out_specs=row_blk,
        out_shape=jax.ShapeDtypeStruct((t, d), F32),
        compiler_params=_params(1),
        name="ffn_ln3",
    )(x2, wg, wu, wd, g, b)


def _rope_lane_order(a):
    lead = a.shape[:-1]
    a = a.reshape(lead + (DA_HEADS, 2, 2, DA_DHEAD // 2))
    return jnp.swapaxes(a, -3, -2).reshape(lead + (-1,))


def kernel(x, mem, positions, w_in, b_in, conv_w, conv_b, ml_norm_w, lam_q1, lam_k1, lam_q2, lam_k2, da_norm_w, w_proj_a, w_proj_b, w_mix_out, ln1_g, ln1_b, w_xq, w_xk, w_xv, w_xo, ln2_g, ln2_b, w_ffn_gate, w_ffn_up, w_ffn_down, ln3_g, ln3_b):
    batch, seq, d = x.shape
    mem_len = mem.shape[1]
    depth = w_in.shape[0]
    assert depth == DEPTH and seq % ROW_TILE == 0 and seq % ML_CHUNK == 0 and seq % ATT_TILE == 0
    t = batch * seq

    ml_qk_w = 2 * ML_HEADS * ML_DQK
    ml_v_w = ML_HEADS * ML_DV
    da_w = DA_HEADS * DA_DV
    sizes = (ml_qk_w, ml_v_w, ml_v_w, 2 * ML_HEADS, da_w, da_w, da_w, 2 * d)
    offs = np.concatenate([[0], np.cumsum(sizes)])
    assert ml_qk_w == d and ml_v_w == d and da_w == d and offs[-1] == w_in.shape[2]
    half = DA_DHEAD // 2
    invf = ROPE_THETA ** (-jnp.arange(half, dtype=F32) / half)
    invf = jnp.tile(invf, V7X_LANES // half).reshape(1, V7X_LANES)

    x2 = x.reshape(t, d)
    pos2 = jnp.repeat(positions.reshape(t // SLAB, ROPE_PACK, SLAB // ROPE_PACK).transpose(0, 2, 1),
                      V7X_LANES // ROPE_PACK, axis=2).reshape(t // ROPE_PACK, V7X_LANES)
    mem2 = mem.reshape(batch * mem_len, d)

    for l in range(depth):
        def piece(a, j):
            return a[..., offs[j]:offs[j + 1]]

        def pack(a):
            pad = jnp.zeros(a.shape[:-1] + (V7X_LANES - 2 * ML_HEADS,), a.dtype)
            return jnp.concatenate(
                [piece(a, 0), _rope_lane_order(piece(a, 4)), _rope_lane_order(piece(a, 5)),
                 piece(a, 7), piece(a, 3), pad], axis=-1)

        w_bf = w_in[l].astype(BF16)
        w_packed = pack(w_bf)
        b_packed = pack(b_in[l].reshape(1, -1))
        assert w_packed.shape[1] == _W_PACKED

        def with_ones(j, heads, dv, ones):
            wt = jnp.pad(piece(w_bf, j).T.reshape(heads, dv, d), ((0, 0), (0, ones), (0, 0)))
            bt = jnp.pad(piece(b_in[l], j).reshape(heads, dv), ((0, 0), (0, ones)),
                         constant_values=1.0)
            return wt.reshape(-1, d), bt.reshape(-1)

        w_mvt, b_mvt = with_ones(1, ML_HEADS, ML_DV, ML_ONES_ROWS)
        w_dvt, b_dvt = with_ones(6, DA_HEADS, DA_DV, ATT_ONES_ROWS)
        gpad = _GT_ROWS - 2 * ML_HEADS
        w_t = jnp.concatenate([w_mvt, piece(w_bf, 2).T, w_dvt,
                               jnp.pad(piece(w_bf, 3).T, ((0, gpad), (0, 0)))], axis=0)
        b_t = jnp.concatenate([b_mvt, piece(b_in[l], 2), b_dvt,
                               jnp.pad(piece(b_in[l], 3), (0, gpad))]).reshape(-1, 1)
        assert w_t.shape[0] == _WT_ROWS

        qk, dq, dk, gates, gif, mvt, ogt, dvt, gt = _in_proj(
            x2, pos2, w_packed, b_packed, conv_w[l], conv_b[l].reshape(1, -1), invf,
            w_t, b_t, seq)

        nw_lanes = jnp.broadcast_to(ml_norm_w[l].reshape(-1, 1), (ML_HEADS * ML_DV, V7X_LANES))
        ha = _mlstm(qk, mvt, ogt, gif, gt, nw_lanes, batch, seq)

        lam_init = 0.8 - 0.6 * math.exp(-0.3 * l)
        lam_vecs = [a[l].reshape(1, -1) for a in (lam_q1, lam_k1, lam_q2, lam_k2)]
        ob = _diff_attn(dq, dk, dvt, lam_vecs, da_norm_w[l].reshape(1, -1), batch, seq, lam_init)

        x2 = _merge(x2, ha, ob, gates, w_proj_a[l].astype(BF16), w_proj_b[l].astype(BF16),
                    w_mix_out[l].astype(BF16), ln1_g[l].reshape(1, -1), ln1_b[l].reshape(1, -1))

        kx, vx = _xattn_kv(mem2, w_xk[l].astype(BF16), w_xv[l].astype(BF16), mem_len)
        x2 = _xattn(x2, kx, vx, w_xq[l].astype(BF16), w_xo[l].astype(BF16),
                    ln2_g[l].reshape(1, -1), ln2_b[l].reshape(1, -1), seq, mem_len)

        x2 = _ffn(x2, w_ffn_gate[l].astype(BF16), w_ffn_up[l].astype(BF16),
                  w_ffn_down[l].astype(BF16), ln3_g[l].reshape(1, -1), ln3_b[l].reshape(1, -1))

    return x2.reshape(batch, seq, d)
```

```python
import functools
import math

import numpy as np
import jax
import jax.numpy as jnp
from jax import lax
from jax.experimental import pallas as pl
from jax.experimental.pallas import tpu as pltpu

ML_HEADS = 4
ML_DQK = 128
ML_DV = 256
ML_CONV = 4
DA_HEADS = 8
DA_DHEAD = 64
DA_DV = 2 * DA_DHEAD
XA_HEADS = 4
ROPE_THETA = 10000.0
DEPTH = 1
ALPHA = (2.0 * DEPTH) ** 0.25
LN_EPS = 1e-5

V7X_LANES = 128
V7X_SUBLANES = 8
V7X_VMEM_LIMIT = 56 * 1024 * 1024

ROW_TILE = 512
ROW_TILE_WIDE = 1024
ROW_SUB = 256
SLAB = 256
ML_CHUNK = SLAB
ML_ONES_ROWS = 16
ATT_TILE = SLAB
ATT_HEADS_PER_STEP = 8
ATT_Q_TILES_PER_STEP = 4
ATT_SCORE_BUFFERS = 4
ATT_SCORE_LOOKAHEAD = 2
ATT_ONES_ROWS = 16
LOG2_E = math.log2(math.e)
ROPE_PACK = V7X_LANES // (DA_DHEAD // 2)

F32 = jnp.float32
BF16 = jnp.bfloat16


def _dot(a, b):
    return jnp.dot(a, b, preferred_element_type=F32)


def _dot_nt(a, b):
    return lax.dot_general(a, b, (((1,), (1,)), ((), ())), preferred_element_type=F32)


def _dot_tn(a, b):
    return lax.dot_general(a, b, (((0,), (0,)), ((), ())), preferred_element_type=F32)


def _layer_norm(z, g, b):
    mu = jnp.mean(z, axis=-1, keepdims=True)
    zc = z - mu
    var = jnp.mean(zc * zc, axis=-1, keepdims=True)
    return zc * lax.rsqrt(var + LN_EPS) * g + b


def _whole_vmem():
    return pl.BlockSpec(memory_space=pltpu.VMEM)


def _params(n_axes):
    return pltpu.CompilerParams(dimension_semantics=("arbitrary",) * n_axes,
                                vmem_limit_bytes=V7X_VMEM_LIMIT)


_C_QK, _C_DQ, _C_DK, _C_GATE, _C_IF = (0, 1024, 2048, 3072, 5120)
_W_PACKED = _C_IF + V7X_LANES
_MVT_ROWS = ML_HEADS * (ML_DV + ML_ONES_ROWS)
_DVT_ROWS = DA_HEADS * (DA_DV + ATT_ONES_ROWS)
_GT_ROWS = 16
_R_MV, _R_OG, _R_DV, _R_GT = 0, _MVT_ROWS, _MVT_ROWS + 1024, _MVT_ROWS + 1024 + _DVT_ROWS
_WT_ROWS = _R_GT + _GT_ROWS


def _in_proj_kernel(x_ref, pos_ref, w_ref, b_ref, cw_ref, cb_ref, invf_ref, wt_ref, bt_ref,
                    qk_ref, dq_ref, dk_ref, gate_ref, gif_ref, mvt_ref, ogt_ref, dvt_ref, gt_ref,
                    pqk_ref, pdq_ref, pdk_ref, pg1_ref, pg2_ref, rt_ref, *, tiles_per_seq):
    tm = x_ref.shape[0]
    d = qk_ref.shape[1]
    half = d // 2

    halo = V7X_SUBLANES
    bqk = b_ref[:, _C_QK:_C_QK + d]

    @pl.when(pl.program_id(0) % tiles_per_seq == 0)
    def _():
        pqk_ref[0:halo, :] = jnp.broadcast_to(-bqk, (halo, d))

    taps = [cw_ref[j:j + 1, :] for j in range(ML_CONV)]
    conv_bias = cb_ref[...] + bqk * (taps[0] + taps[1] + taps[2] + taps[3])
    lane = lax.broadcasted_iota(jnp.int32, (1, V7X_LANES), 1)
    rope_w = V7X_LANES // ROPE_PACK

    n_sub = tm // SLAB

    def proj_to(dst_ref, xb, c0):
        dst_ref[...] = _dot(xb, w_ref[:, c0:c0 + d])

    def bias(c0, width):
        return b_ref[:, c0:c0 + width]

    xb = x_ref[0:SLAB, :].astype(BF16)
    proj_to(pqk_ref.at[halo:halo + SLAB], xb, _C_QK)
    for c in range(n_sub):
        rs = slice(c * SLAB, (c + 1) * SLAB)
        proj_to(pdq_ref, xb, _C_DQ)
        proj_to(pdk_ref, xb, _C_DK)
        rt_ref[...] = _dot_nt(wt_ref[...], xb)

        acc = conv_bias
        for j in range(ML_CONV):
            back = ML_CONV - 1 - j
            acc = acc + pqk_ref[halo - back:halo - back + SLAB, :] * taps[j]
        pqk_ref[0:halo, :] = pqk_ref[SLAB:SLAB + halo, :]
        y = acc * jax.nn.sigmoid(acc)
        qk_ref[rs, 0:half] = (y[:, 0:half] * (ML_DQK ** -0.5)).astype(BF16)
        qk_ref[rs, half:d] = y[:, half:d].astype(BF16)

        proj_to(pg1_ref, xb, _C_GATE)
        proj_to(pg2_ref, xb, _C_GATE + d)
        gif_ref[rs, :] = _dot(xb, w_ref[:, _C_IF:_C_IF + V7X_LANES]) + bias(_C_IF, V7X_LANES)

        pk = SLAB // ROPE_PACK
        ang = pos_ref[c * pk:(c + 1) * pk, :].astype(F32) * invf_ref[...]
        cos_p = jnp.cos(ang)
        sin_p = jnp.sin(ang)

        def spread(v):
            out = []
            for g in range(ROPE_PACK):
                u = v if g == 0 else pltpu.roll(v, V7X_LANES - g * rope_w, 1)
                u = jnp.where(lane < rope_w, u, pltpu.roll(u, rope_w, 1))
                out.append(jnp.where(lane < 2 * rope_w, u, pltpu.roll(u, 2 * rope_w, 1)))
            return jnp.concatenate(out, axis=0)

        cosv = spread(cos_p)
        sinv = spread(sin_p)
        ssin = jnp.where(lane < V7X_LANES // 2, -sinv, sinv)

        def rope_store(out_ref, p_ref, c0, scale):
            for h in range(d // V7X_LANES):
                cols = slice(h * V7X_LANES, (h + 1) * V7X_LANES)
                xh = p_ref[:, cols] + b_ref[:, c0 + h * V7X_LANES:c0 + (h + 1) * V7X_LANES]
                r = xh * cosv + pltpu.roll(xh, V7X_LANES // 2, 1) * ssin
                out_ref[rs, cols] = (r * scale).astype(BF16)

        rope_store(dq_ref, pdq_ref, _C_DQ, DA_DHEAD ** -0.5 * LOG2_E)
        rope_store(dk_ref, pdk_ref, _C_DK, 1.0)

        mvt_ref[c] = (rt_ref[_R_MV:_R_OG, :] + bt_ref[_R_MV:_R_OG, :]).astype(BF16)
        ogt_ref[c] = jax.nn.sigmoid(rt_ref[_R_OG:_R_DV, :] + bt_ref[_R_OG:_R_DV, :]).astype(BF16)
        dvt_ref[c] = (rt_ref[_R_DV:_R_GT, :] + bt_ref[_R_DV:_R_GT, :]).astype(BF16)
        gt_ref[c] = rt_ref[_R_GT:_WT_ROWS, :] + bt_ref[_R_GT:_WT_ROWS, :]

        if c + 1 < n_sub:
            xb = x_ref[(c + 1) * SLAB:(c + 2) * SLAB, :].astype(BF16)
            proj_to(pqk_ref.at[halo:halo + SLAB], xb, _C_QK)

        gate_ref[rs, 0:d] = jax.nn.sigmoid(pg1_ref[...] + bias(_C_GATE, d)).astype(BF16)
        gate_ref[rs, d:2 * d] = jax.nn.sigmoid(pg2_ref[...] + bias(_C_GATE + d, d)).astype(BF16)


def _in_proj(x2, pos2, w_packed, b_packed, conv_w, conv_b, invf, w_t, b_t, seq):
    t, d = x2.shape
    tm = ROW_TILE
    row_blk = lambda w: pl.BlockSpec((tm, w), lambda i: (i, 0))
    slab_blk = lambda r: pl.BlockSpec((tm // SLAB, r, SLAB), lambda i: (i, 0, 0))
    row_out = lambda w, dt: jax.ShapeDtypeStruct((t, w), dt)
    slab_out = lambda r, dt: jax.ShapeDtypeStruct((t // SLAB, r, SLAB), dt)
    return pl.pallas_call(
        functools.partial(_in_proj_kernel, tiles_per_seq=seq // tm),
        grid=(t // tm,),
        in_specs=[row_blk(d), pl.BlockSpec((tm // ROPE_PACK, V7X_LANES), lambda i: (i, 0))]
        + [_whole_vmem()] * 7,
        out_specs=[row_blk(d), row_blk(d), row_blk(d), row_blk(2 * d), row_blk(V7X_LANES),
                   slab_blk(_MVT_ROWS), slab_blk(d), slab_blk(_DVT_ROWS), slab_blk(_GT_ROWS)],
        out_shape=[row_out(d, BF16), row_out(d, BF16), row_out(d, BF16), row_out(2 * d, BF16),
                   row_out(V7X_LANES, F32), slab_out(_MVT_ROWS, BF16), slab_out(d, BF16),
                   slab_out(_DVT_ROWS, BF16), slab_out(_GT_ROWS, F32)],
        scratch_shapes=[pltpu.VMEM((V7X_SUBLANES + SLAB, d), F32)]
        + [pltpu.VMEM((SLAB, d), F32)] * 4 + [pltpu.VMEM((_WT_ROWS, SLAB), F32)],
        compiler_params=_params(1),
        name="in_proj",
    )(x2, pos2, w_packed, b_packed, conv_w, conv_b, invf, w_t, b_t)


def _split3_bf16(a):
    hi = a.astype(BF16)
    r1 = a - hi.astype(F32)
    mid = r1.astype(BF16)
    lo = (r1 - mid.astype(F32)).astype(BF16)
    return hi, mid, lo


def _mlstm_kernel(qk_ref, vt_ref, ogt_ref, gif_ref, gt_ref, nw_ref, out_ref,
                  c_ref, m_ref, qkt_ref, cq_ref, bcr_ref, z_ref):
    seq = qk_ref.shape[0]
    L = ML_CHUNK
    vrows = ML_DV + ML_ONES_ROWS
    c_ref[...] = jnp.zeros_like(c_ref)
    m_ref[...] = jnp.zeros_like(m_ref)

    si = lax.broadcasted_iota(jnp.int32, (L, L), 0)
    ti = lax.broadcasted_iota(jnp.int32, (L, L), 1)
    tri_lo = jnp.where(ti <= si, 1.0, 0.0).astype(BF16)
    tri_up = jnp.where(si <= ti, 1.0, 0.0).astype(BF16)
    causal_t = si <= ti
    nw = jnp.concatenate([nw_ref[...]] * (L // V7X_LANES), axis=1)

    n_chunks = seq // L

    def rows_of(c):
        return pl.ds(pl.multiple_of(c * L, L), L)

    def gate_sums(c):
        r3 = _split3_bf16(jax.nn.log_sigmoid(gt_ref[c]))
        bcr_ref[c % 2] = _dot(r3[0], tri_up) + _dot(r3[1], tri_up) + _dot(r3[2], tri_up)
        g = gif_ref[rows_of(c), :]
        c3 = _split3_bf16(jax.nn.log_sigmoid(g))
        bc_col = _dot(tri_lo, c3[0]) + _dot(tri_lo, c3[1]) + _dot(tri_lo, c3[2])
        z_ref[c % 2] = g - pltpu.roll(bc_col, V7X_LANES - ML_HEADS, 1)

    def key_query(c, h):
        q = qk_ref[rows_of(c), h * ML_DQK:(h + 1) * ML_DQK]
        k = qk_ref[rows_of(c), (ML_HEADS + h) * ML_DQK:(ML_HEADS + h + 1) * ML_DQK]
        qkt_ref[h] = _dot_nt(k, q)

    gate_sums(0)
    for h in range(ML_HEADS):
        key_query(0, h)

    def chunk(c, carry):
        rows = rows_of(c)
        nxt = jnp.minimum(c + 1, n_chunks - 1)
        gt = gt_ref[c]
        bc_row = bcr_ref[c % 2]
        z = z_ref[c % 2]
        for h in range(ML_HEADS):
            q = qk_ref[rows, h * ML_DQK:(h + 1) * ML_DQK]
            cq_ref[h] = _dot_nt(c_ref[h].astype(BF16), q)
        gate_sums(nxt)
        for h in range(ML_HEADS):
            b_row = bc_row[ML_HEADS + h:ML_HEADS + h + 1, :]
            i_row = gt[h:h + 1, :]
            r_col = z[:, h:h + 1]
            m_prev = m_ref[h]
            dmat = jnp.where(causal_t, b_row + r_col, -jnp.inf)
            inter = b_row + m_prev
            m_row = jnp.maximum(inter, jnp.max(dmat, axis=0, keepdims=True))
            w_inter = jnp.exp(inter - m_row)
            pt = (qkt_ref[h] * jnp.exp(dmat - m_row)).astype(BF16)
            key_query(nxt, h)
            vt = vt_ref[c, h * vrows:(h + 1) * vrows, :]
            num = w_inter * cq_ref[h] + _dot(vt, pt)
            den = num[ML_DV:ML_DV + 1, :]
            rden = 1.0 / jnp.maximum(jnp.abs(den), jnp.exp(-m_row))
            b_last = b_row[:, L - 1:L]
            g_row = b_last - b_row + i_row
            m_new = jnp.maximum(b_last + m_prev, jnp.max(g_row, axis=1, keepdims=True))
            decay = jnp.exp(b_last + m_prev - m_new)
            ws = jnp.exp(g_row - m_new)
            k = qk_ref[rows, (ML_HEADS + h) * ML_DQK:(ML_HEADS + h + 1) * ML_DQK]
            c_ref[h] = decay * c_ref[h] + _dot((vt.astype(F32) * ws).astype(BF16), k)
            m_ref[h] = m_new
            xn = num[0:ML_DV, :]
            xc = xn - jnp.mean(xn, axis=0, keepdims=True)
            var = jnp.mean(xc * xc, axis=0, keepdims=True)
            scale = rden * lax.rsqrt(var * rden * rden + LN_EPS)
            hn = xc * scale * nw[h * ML_DV:(h + 1) * ML_DV, :]
            gate = ogt_ref[c, h * ML_DV:(h + 1) * ML_DV, :].astype(F32)
            out_ref[rows, h * ML_DV:(h + 1) * ML_DV] = (gate * hn).T.astype(BF16)
        return carry

    lax.fori_loop(0, seq // L, chunk, 0)


def _mlstm(qk, mvt, ogt, gif, gt, nw_lanes, batch, seq):
    t, d = qk.shape
    L = ML_CHUNK
    nc = seq // L
    vrows = ML_DV + ML_ONES_ROWS
    seq_blk = lambda w: pl.BlockSpec((seq, w), lambda b: (b, 0))
    slab_blk = lambda r: pl.BlockSpec((nc, r, L), lambda b: (b, 0, 0))
    return pl.pallas_call(
        _mlstm_kernel,
        grid=(batch,),
        in_specs=[seq_blk(d), slab_blk(_MVT_ROWS), slab_blk(d), seq_blk(V7X_LANES),
                  slab_blk(_GT_ROWS), _whole_vmem()],
        out_specs=seq_blk(d),
        out_shape=jax.ShapeDtypeStruct((t, d), BF16),
        scratch_shapes=[pltpu.VMEM((ML_HEADS, vrows, ML_DQK), F32),
                        pltpu.VMEM((ML_HEADS, 1, 1), F32),
                        pltpu.VMEM((ML_HEADS, L, L), F32),
                        pltpu.VMEM((ML_HEADS, vrows, L), F32),
                        pltpu.VMEM((2, _GT_ROWS, L), F32),
                        pltpu.VMEM((2, L, V7X_LANES), F32)],
        compiler_params=_params(1),
        name="mlstm",
    )(qk, mvt, ogt, gif, gt, nw_lanes)


def _diff_attn_kernel(lq1_ref, lk1_ref, lq2_ref, lk2_ref, q_ref, k_ref, vt_ref, nw_ref,
                      o_ref, q12_ref, st_ref, mx_ref, m_ref, acc_ref, *, lam_init):
    tq = tk = ATT_TILE
    groups = q_ref.shape[0] // tq
    heads = q_ref.shape[1] // DA_DV
    n_slots = groups * heads
    i = pl.program_id(2)
    lam = (jnp.exp(jnp.sum(lq1_ref[...] * lk1_ref[...], axis=1, keepdims=True))
           - jnp.exp(jnp.sum(lq2_ref[...] * lk2_ref[...], axis=1, keepdims=True)) + lam_init)

    feat = lax.broadcasted_iota(jnp.int32, (DA_DV, tq), 0)
    first = (feat & (DA_DHEAD // 2)) == 0
    for s in range(n_slots):
        g, h = divmod(s, heads)
        qt = q_ref[g * tq:(g + 1) * tq, h * DA_DV:(h + 1) * DA_DV].astype(F32).T
        q12_ref[s, :, 0:tq] = jnp.where(first, qt, 0.0).astype(BF16)
        q12_ref[s, :, tq:2 * tq] = jnp.where(first, 0.0, qt).astype(BF16)

    vrows = DA_DV + ATT_ONES_ROWS
    n_st = st_ref.shape[0]
    ahead = ATT_SCORE_LOOKAHEAD
    assert ahead < n_st and n_slots % n_st == 0

    def scores(pos, j, s):
        h = s % heads
        rows = pl.ds(pl.multiple_of(j * tk, tk), tk)
        st = _dot(k_ref[rows, h * DA_DV:(h + 1) * DA_DV], q12_ref[s])
        st_ref[pos % n_st] = st
        mx_ref[pos % n_st] = jnp.max(st, axis=0, keepdims=True)

    def softmax(pos, s, mask_add):
        st = st_ref[pos % n_st]
        if mask_add is None:
            mx = mx_ref[pos % n_st]
        else:
            st = st + mask_add
            mx = jnp.max(st, axis=0, keepdims=True)
        m_prev = m_ref[s]
        m_new = jnp.maximum(m_prev, mx)
        alpha = jnp.exp2(m_prev - m_new)
        pt = jnp.exp2(st - m_new)
        m_ref[s] = m_new
        return alpha, pt.astype(BF16)

    def accumulate(j, s, alpha, pt):
        h = s % heads
        acc_ref[s] = alpha * acc_ref[s] + _dot(vt_ref[j, h * vrows:(h + 1) * vrows, :], pt)

    base = groups * i
    tail = [(u, s, s // heads == u) for u in range(groups)
            for s in range(n_slots) if s // heads >= u]

    for pos in range(ahead):
        scores(pos, 0, pos)
    m_ref[...] = jnp.full_like(m_ref, -jnp.inf)
    acc_ref[...] = jnp.zeros_like(acc_ref)

    def body(j, carry):
        for s in range(n_slots):
            alpha, pt = softmax(s, s, None)
            if s + ahead < n_slots:
                scores(s + ahead, j, s + ahead)
            else:
                scores(s + ahead, j + 1, s + ahead - n_slots)
            accumulate(j, s, alpha, pt)
        return carry

    lax.fori_loop(0, base, body, 0)
    kj = lax.broadcasted_iota(jnp.int32, (tk, 2 * tq), 0)
    qi = lax.broadcasted_iota(jnp.int32, (tk, 2 * tq), 1) & (tq - 1)
    causal_add = jnp.where(kj <= qi, 0.0, -jnp.inf)
    for pos, (u, s, diagonal) in enumerate(tail):
        alpha, pt = softmax(pos, s, causal_add if diagonal else None)
        if pos + ahead < len(tail):
            u2, s2, _ = tail[pos + ahead]
            scores(pos + ahead, base + u2, s2)
        accumulate(base + u, s, alpha, pt)

    for s in range(n_slots):
        g, h = divmod(s, heads)
        acc = acc_ref[s, 0:DA_DV, :]
        r = 1.0 / acc_ref[s, DA_DV:DA_DV + 1, :]
        ot = acc[:, 0:tq] * r[:, 0:tq] - acc[:, tq:2 * tq] * (lam * r[:, tq:2 * tq])
        ms = jnp.mean(ot * ot, axis=0, keepdims=True)
        o = (ot * (lax.rsqrt(ms + LN_EPS) * (1.0 - lam_init))).T
        o_ref[g * tq:(g + 1) * tq, h * DA_DV:(h + 1) * DA_DV] = (
            o * nw_ref[:, h * DA_DV:(h + 1) * DA_DV]).astype(BF16)


def _diff_attn(dq, dk, dvt, lam_vecs, da_norm_w, batch, seq, lam_init):
    t, d = dq.shape
    tq = ATT_TILE
    assert tq & (tq - 1) == 0
    nq = seq // tq
    hb = ATT_HEADS_PER_STEP
    qt = ATT_Q_TILES_PER_STEP
    assert nq % qt == 0
    steps = nq // qt
    slots = qt * hb
    w = hb * DA_DV
    vrows = DA_DV + ATT_ONES_ROWS
    lam_spec = pl.BlockSpec((1, DA_DHEAD), lambda b, g, i: (0, 0))
    q_spec = pl.BlockSpec((qt * tq, w), lambda b, g, i: (b * steps + i, g))
    return pl.pallas_call(
        functools.partial(_diff_attn_kernel, lam_init=lam_init),
        grid=(batch, DA_HEADS // hb, steps),
        in_specs=[lam_spec] * 4 + [q_spec,
                                   pl.BlockSpec((seq, w), lambda b, g, i: (b, g)),
                                   pl.BlockSpec((nq, hb * vrows, tq), lambda b, g, i: (b, g, 0)),
                                   pl.BlockSpec((1, w), lambda b, g, i: (0, g))],
        out_specs=q_spec,
        out_shape=jax.ShapeDtypeStruct((t, d), BF16),
        scratch_shapes=[pltpu.VMEM((slots, DA_DV, 2 * tq), BF16),
                        pltpu.VMEM((ATT_SCORE_BUFFERS, tq, 2 * tq), F32),
                        pltpu.VMEM((ATT_SCORE_BUFFERS, 1, 2 * tq), F32),
                        pltpu.VMEM((slots, 1, 2 * tq), F32),
                        pltpu.VMEM((slots, vrows, 2 * tq), F32)],
        compiler_params=_params(3),
        name="diff_attn",
    )(*lam_vecs, dq, dk, dvt, da_norm_w)


def _row_subs(tm):
    return [slice(r * ROW_SUB, (r + 1) * ROW_SUB) for r in range(tm // ROW_SUB)]


def _pipeline_subs(subs, first_matmuls, middle, second_matmul, finish):
    a = first_matmuls(subs[0])
    c_prev = None
    for r, s in enumerate(subs):
        a_next = first_matmuls(subs[r + 1]) if r + 1 < len(subs) else None
        c = second_matmul(middle(s, a))
        if c_prev is not None:
            finish(subs[r - 1], c_prev)
        a, c_prev = a_next, c
    finish(subs[-1], c_prev)


def _merge_kernel(x_ref, ha_ref, ob_ref, gate_ref, wa_ref, wb_ref, wm_ref, g_ref, b_ref, o_ref):
    d = x_ref.shape[1]

    def branch_projections(s):
        return _dot(ha_ref[s, :], wa_ref[...]), _dot(ob_ref[s, :], wb_ref[...])

    def gated_merge(s, yab):
        ya, yb = yab
        mixin = gate_ref[s, 0:d].astype(F32) * ya + gate_ref[s, d:2 * d].astype(F32) * yb
        return mixin.astype(BF16)

    def finish(s, mix):
        o_ref[s, :] = _layer_norm(ALPHA * x_ref[s, :] + mix, g_ref[...], b_ref[...])

    _pipeline_subs(_row_subs(x_ref.shape[0]), branch_projections, gated_merge,
                   lambda mixin: _dot(mixin, wm_ref[...]), finish)


def _merge(x2, ha, ob, gates, wa, wb, wm, g, b):
    t, d = x2.shape
    tm = ROW_TILE_WIDE
    row_blk = lambda w: pl.BlockSpec((tm, w), lambda i: (i, 0))
    return pl.pallas_call(
        _merge_kernel,
        grid=(t // tm,),
        in_specs=[row_blk(d), row_blk(d), row_blk(d), row_blk(2 * d)] + [_whole_vmem()] * 5,
        out_specs=row_blk(d),
        out_shape=jax.ShapeDtypeStruct((t, d), F32),
        compiler_params=_params(1),
        name="merge_ln1",
    )(x2, ha, ob, gates, wa, wb, wm, g, b)


def _xattn_kv_kernel(mem_ref, wk_ref, wv_ref, k_ref, v_ref):
    mb = mem_ref[...].astype(BF16)
    k_ref[...] = _dot(mb, wk_ref[...]).astype(BF16)
    v_ref[...] = _dot(mb, wv_ref[...]).astype(BF16)


def _xattn_kv(mem2, wk, wv, mem_len):
    t, d = mem2.shape
    blk = pl.BlockSpec((mem_len, d), lambda b: (b, 0))
    return pl.pallas_call(
        _xattn_kv_kernel,
        grid=(t // mem_len,),
        in_specs=[blk, _whole_vmem(), _whole_vmem()],
        out_specs=[blk, blk],
        out_shape=[jax.ShapeDtypeStruct((t, d), BF16)] * 2,
        compiler_params=_params(1),
        name="xattn_kv",
    )(mem2, wk, wv)


def _xattn_kernel(x_ref, k_ref, v_ref, wq_ref, wo_ref, g_ref, b_ref, o_ref):
    d = x_ref.shape[1]
    dh = d // XA_HEADS
    subs = _row_subs(x_ref.shape[0])
    head_cols = [slice(h * dh, (h + 1) * dh) for h in range(XA_HEADS)]

    def project_q(s):
        return (_dot(x_ref[s, :].astype(BF16), wq_ref[...]) * (dh ** -0.5)).astype(BF16)

    def scores(q):
        return [_dot_nt(q[:, c], k_ref[:, c]) for c in head_cols]

    def softmax(sc):
        e = jnp.exp(sc - jnp.max(sc, axis=1, keepdims=True))
        return (e / jnp.sum(e, axis=1, keepdims=True)).astype(BF16)

    def finish(r, xo):
        o_ref[subs[r], :] = _layer_norm(ALPHA * x_ref[subs[r], :] + xo, g_ref[...], b_ref[...])

    n = len(subs)
    q = project_q(subs[0])
    sc = scores(q)
    xo_prev = None
    for r in range(n):
        if r + 1 < n:
            q = project_q(subs[r + 1])
        ps = [softmax(s_h) for s_h in sc]
        o = jnp.concatenate([_dot(p, v_ref[:, c]).astype(BF16) for p, c in zip(ps, head_cols)],
                            axis=1)
        if r + 1 < n:
            sc = scores(q)
        xo = _dot(o, wo_ref[...])
        if xo_prev is not None:
            finish(r - 1, xo_prev)
        xo_prev = xo
    finish(n - 1, xo_prev)


def _xattn(x1, kx, vx, wq, wo, g, b, seq, mem_len):
    t, d = x1.shape
    tm = ROW_TILE_WIDE
    tiles_per_seq = seq // tm
    row_blk = pl.BlockSpec((tm, d), lambda i: (i, 0))
    kv_blk = pl.BlockSpec((mem_len, d), lambda i: (i // tiles_per_seq, 0))
    return pl.pallas_call(
        _xattn_kernel,
        grid=(t // tm,),
        in_specs=[row_blk, kv_blk, kv_blk] + [_whole_vmem()] * 4,
        out_specs=row_blk,
        out_shape=jax.ShapeDtypeStruct((t, d), F32),
        compiler_params=_params(1),
        name="xattn_ln2",
    )(x1, kx, vx, wq, wo, g, b)


def _ffn_kernel(x_ref, wg_ref, wu_ref, wd_ref, g_ref, b_ref, o_ref):
    def gate_up(s):
        xb = x_ref[s, :].astype(BF16)
        return _dot(xb, wg_ref[...]), _dot(xb, wu_ref[...])

    def hidden(s, gu):
        gt, up = gu
        return (gt * jax.nn.sigmoid(gt) * up).astype(BF16)

    def finish(s, dn):
        o_ref[s, :] = _layer_norm(ALPHA * x_ref[s, :] + dn, g_ref[...], b_ref[...])

    _pipeline_subs(_row_subs(x_ref.shape[0]), gate_up, hidden,
                   lambda hid: _dot(hid, wd_ref[...]), finish)


def _ffn(x2, wg, wu, wd, g, b):
    t, d = x2.shape
    tm = ROW_TILE_WIDE
    row_blk = pl.BlockSpec((tm, d), lambda i: (i, 0))
    return pl.pallas_call(
        _ffn_kernel,
        grid=(t // tm,),
        in_specs=[row_blk] + [_whole_vmem()] * 5,
        out_specs=row_blk,
        out_shape=jax.ShapeDtypeStruct((t, d), F32),
        compiler_params=_params(1),
        name="ffn_ln3",
    )(x2, wg, wu, wd, g, b)


def _rope_lane_order(a):
    lead = a.shape[:-1]
    a = a.reshape(lead + (DA_HEADS, 2, 2, DA_DHEAD // 2))
    return jnp.swapaxes(a, -3, -2).reshape(lead + (-1,))


def kernel(x, mem, positions, w_in, b_in, conv_w, conv_b, ml_norm_w, lam_q1, lam_k1, lam_q2, lam_k2, da_norm_w, w_proj_a, w_proj_b, w_mix_out, ln1_g, ln1_b, w_xq, w_xk, w_xv, w_xo, ln2_g, ln2_b, w_ffn_gate, w_ffn_up, w_ffn_down, ln3_g, ln3_b):
    batch, seq, d = x.shape
    mem_len = mem.shape[1]
    depth = w_in.shape[0]
    assert depth == DEPTH and seq % ROW_TILE == 0 and seq % ROW_TILE_WIDE == 0
    assert seq % ML_CHUNK == 0 and seq % ATT_TILE == 0
    t = batch * seq

    ml_qk_w = 2 * ML_HEADS * ML_DQK
    ml_v_w = ML_HEADS * ML_DV
    da_w = DA_HEADS * DA_DV
    sizes = (ml_qk_w, ml_v_w, ml_v_w, 2 * ML_HEADS, da_w, da_w, da_w, 2 * d)
    offs = np.concatenate([[0], np.cumsum(sizes)])
    assert ml_qk_w == d and ml_v_w == d and da_w == d and offs[-1] == w_in.shape[2]
    half = DA_DHEAD // 2
    invf = ROPE_THETA ** (-jnp.arange(half, dtype=F32) / half)
    invf = jnp.tile(invf, V7X_LANES // half).reshape(1, V7X_LANES)

    x2 = x.reshape(t, d)
    pos2 = jnp.repeat(positions.reshape(t // SLAB, ROPE_PACK, SLAB // ROPE_PACK).transpose(0, 2, 1),
                      V7X_LANES // ROPE_PACK, axis=2).reshape(t // ROPE_PACK, V7X_LANES)
    mem2 = mem.reshape(batch * mem_len, d)

    for l in range(depth):
        def piece(a, j):
            return a[..., offs[j]:offs[j + 1]]

        def pack(a):
            pad = jnp.zeros(a.shape[:-1] + (V7X_LANES - 2 * ML_HEADS,), a.dtype)
            return jnp.concatenate(
                [piece(a, 0), _rope_lane_order(piece(a, 4)), _rope_lane_order(piece(a, 5)),
                 piece(a, 7), piece(a, 3), pad], axis=-1)

        w_bf = w_in[l].astype(BF16)
        w_packed = pack(w_bf)
        b_packed = pack(b_in[l].reshape(1, -1))
        assert w_packed.shape[1] == _W_PACKED

        def with_ones(j, heads, dv, ones):
            wt = jnp.pad(piece(w_bf, j).T.reshape(heads, dv, d), ((0, 0), (0, ones), (0, 0)))
            bt = jnp.pad(piece(b_in[l], j).reshape(heads, dv), ((0, 0), (0, ones)),
                         constant_values=1.0)
            return wt.reshape(-1, d), bt.reshape(-1)

        w_mvt, b_mvt = with_ones(1, ML_HEADS, ML_DV, ML_ONES_ROWS)
        w_dvt, b_dvt = with_ones(6, DA_HEADS, DA_DV, ATT_ONES_ROWS)
        gpad = _GT_ROWS - 2 * ML_HEADS
        w_t = jnp.concatenate([w_mvt, piece(w_bf, 2).T, w_dvt,
                               jnp.pad(piece(w_bf, 3).T, ((0, gpad), (0, 0)))], axis=0)
        b_t = jnp.concatenate([b_mvt, piece(b_in[l], 2), b_dvt,
                               jnp.pad(piece(b_in[l], 3), (0, gpad))]).reshape(-1, 1)
        assert w_t.shape[0] == _WT_ROWS

        qk, dq, dk, gates, gif, mvt, ogt, dvt, gt = _in_proj(
            x2, pos2, w_packed, b_packed, conv_w[l], conv_b[l].reshape(1, -1), invf,
            w_t, b_t, seq)

        nw_lanes = jnp.broadcast_to(ml_norm_w[l].reshape(-1, 1), (ML_HEADS * ML_DV, V7X_LANES))
        ha = _mlstm(qk, mvt, ogt, gif, gt, nw_lanes, batch, seq)

        lam_init = 0.8 - 0.6 * math.exp(-0.3 * l)
        lam_vecs = [a[l].reshape(1, -1) for a in (lam_q1, lam_k1, lam_q2, lam_k2)]
        ob = _diff_attn(dq, dk, dvt, lam_vecs, da_norm_w[l].reshape(1, -1), batch, seq, lam_init)

        x2 = _merge(x2, ha, ob, gates, w_proj_a[l].astype(BF16), w_proj_b[l].astype(BF16),
                    w_mix_out[l].astype(BF16), ln1_g[l].reshape(1, -1), ln1_b[l].reshape(1, -1))

        kx, vx = _xattn_kv(mem2, w_xk[l].astype(BF16), w_xv[l].astype(BF16), mem_len)
        x2 = _xattn(x2, kx, vx, w_xq[l].astype(BF16), w_xo[l].astype(BF16),
                    ln2_g[l].reshape(1, -1), ln2_b[l].reshape(1, -1), seq, mem_len)

        x2 = _ffn(x2, w_ffn_gate[l].astype(BF16), w_ffn_up[l].astype(BF16),
                  w_ffn_down[l].astype(BF16), ln3_g[l].reshape(1, -1), ln3_b[l].reshape(1, -1))

    return x2.reshape(batch, seq, d)
```

```python
import functools
import math

import numpy as np
import jax
import jax.numpy as jnp
from jax import lax
from jax.experimental import pallas as pl
from jax.experimental.pallas import tpu as pltpu

D_MODEL = 1024
ML_HEADS = 4
ML_DQK = 128
ML_DV = 256
ML_CONV = 4
DA_HEADS = 8
DA_DHEAD = 64
DA_DV = 2 * DA_DHEAD
XA_HEADS = 4
ROPE_THETA = 10000.0
DEPTH = 1
ALPHA = (2.0 * DEPTH) ** 0.25
LN_EPS = 1e-5

V7X_LANES = 128
V7X_SUBLANES = 8
V7X_VMEM_LIMIT = 56 * 1024 * 1024

ROW_TILE = 512
ROW_TILE_WIDE = 1024
ROW_SUB = 256
SLAB = 256
ML_CHUNK = SLAB
ML_ONES_ROWS = 16
ATT_TILE = SLAB
ATT_HEADS_PER_STEP = 8
ATT_Q_TILES_PER_STEP = 4
ATT_SCORE_BUFFERS = 4
ATT_SCORE_LOOKAHEAD = 2
ATT_ONES_ROWS = 16
LOG2_E = math.log2(math.e)
ROPE_PACK = V7X_LANES // (DA_DHEAD // 2)

F32 = jnp.float32
BF16 = jnp.bfloat16


def _dot(a, b):
    return jnp.dot(a, b, preferred_element_type=F32)


def _dot_nt(a, b):
    return lax.dot_general(a, b, (((1,), (1,)), ((), ())), preferred_element_type=F32)


def _layer_norm(z, g, b):
    mu = jnp.mean(z, axis=-1, keepdims=True)
    zc = z - mu
    var = jnp.mean(zc * zc, axis=-1, keepdims=True)
    return zc * lax.rsqrt(var + LN_EPS) * g + b


def _whole_vmem():
    return pl.BlockSpec(memory_space=pltpu.VMEM)


def _params(n_axes):
    return pltpu.CompilerParams(dimension_semantics=("arbitrary",) * n_axes,
                                vmem_limit_bytes=V7X_VMEM_LIMIT)


_C_QK, _C_DQ, _C_DK, _C_GATE, _C_IF = (0, D_MODEL, 2 * D_MODEL, 3 * D_MODEL, 5 * D_MODEL)
_W_PACKED = _C_IF + V7X_LANES
_MVT_ROWS = ML_HEADS * (ML_DV + ML_ONES_ROWS)
_DVT_ROWS = DA_HEADS * (DA_DV + ATT_ONES_ROWS)
_GT_ROWS = 16
_R_MV, _R_OG, _R_DV = 0, _MVT_ROWS, _MVT_ROWS + D_MODEL
_R_GT = _R_DV + _DVT_ROWS
_WT_ROWS = _R_GT + _GT_ROWS


def _in_proj_kernel(x_ref, pos_ref, w_ref, b_ref, cw_ref, cb_ref, invf_ref, wt_ref, bt_ref,
                    qk_ref, dq_ref, dk_ref, gate_ref, gif_ref, mvt_ref, ogt_ref, dvt_ref, gt_ref,
                    pqk_ref, pdq_ref, pdk_ref, pg1_ref, pg2_ref, rt_ref, *, tiles_per_seq):
    tm = x_ref.shape[0]
    d = qk_ref.shape[1]
    half = d // 2

    halo = V7X_SUBLANES
    bqk = b_ref[:, _C_QK:_C_QK + d]

    @pl.when(pl.program_id(0) % tiles_per_seq == 0)
    def _():
        pqk_ref[0:halo, :] = jnp.broadcast_to(-bqk, (halo, d))

    taps = [cw_ref[j:j + 1, :] for j in range(ML_CONV)]
    conv_bias = cb_ref[...] + bqk * (taps[0] + taps[1] + taps[2] + taps[3])
    lane = lax.broadcasted_iota(jnp.int32, (1, V7X_LANES), 1)
    rope_w = V7X_LANES // ROPE_PACK

    n_sub = tm // SLAB

    def proj_to(dst_ref, xb, c0):
        dst_ref[...] = _dot(xb, w_ref[:, c0:c0 + d])

    def bias(c0, width):
        return b_ref[:, c0:c0 + width]

    xb = x_ref[0:SLAB, :].astype(BF16)
    proj_to(pqk_ref.at[halo:halo + SLAB], xb, _C_QK)
    for c in range(n_sub):
        rs = slice(c * SLAB, (c + 1) * SLAB)
        proj_to(pdq_ref, xb, _C_DQ)
        proj_to(pdk_ref, xb, _C_DK)
        rt_ref[...] = _dot_nt(wt_ref[...], xb)

        acc = conv_bias
        for j in range(ML_CONV):
            back = ML_CONV - 1 - j
            acc = acc + pqk_ref[halo - back:halo - back + SLAB, :] * taps[j]
        pqk_ref[0:halo, :] = pqk_ref[SLAB:SLAB + halo, :]
        y = acc * jax.nn.sigmoid(acc)
        qk_ref[rs, 0:half] = (y[:, 0:half] * (ML_DQK ** -0.5)).astype(BF16)
        qk_ref[rs, half:d] = y[:, half:d].astype(BF16)

        proj_to(pg1_ref, xb, _C_GATE)
        proj_to(pg2_ref, xb, _C_GATE + d)
        gif_ref[rs, :] = _dot(xb, w_ref[:, _C_IF:_C_IF + V7X_LANES]) + bias(_C_IF, V7X_LANES)

        pk = SLAB // ROPE_PACK
        ang = pos_ref[c * pk:(c + 1) * pk, :].astype(F32) * invf_ref[...]
        cos_p = jnp.cos(ang)
        sin_p = jnp.sin(ang)

        def spread(v):
            out = []
            for g in range(ROPE_PACK):
                u = v if g == 0 else pltpu.roll(v, V7X_LANES - g * rope_w, 1)
                u = jnp.where(lane < rope_w, u, pltpu.roll(u, rope_w, 1))
                out.append(jnp.where(lane < 2 * rope_w, u, pltpu.roll(u, 2 * rope_w, 1)))
            return jnp.concatenate(out, axis=0)

        cosv = spread(cos_p)
        sinv = spread(sin_p)
        ssin = jnp.where(lane < V7X_LANES // 2, -sinv, sinv)

        def rope_store(out_ref, p_ref, c0, scale):
            for h in range(d // V7X_LANES):
                cols = slice(h * V7X_LANES, (h + 1) * V7X_LANES)
                xh = p_ref[:, cols] + b_ref[:, c0 + h * V7X_LANES:c0 + (h + 1) * V7X_LANES]
                r = xh * cosv + pltpu.roll(xh, V7X_LANES // 2, 1) * ssin
                out_ref[rs, cols] = (r * scale).astype(BF16)

        rope_store(dq_ref, pdq_ref, _C_DQ, DA_DHEAD ** -0.5 * LOG2_E)
        rope_store(dk_ref, pdk_ref, _C_DK, 1.0)

        mvt_ref[c] = (rt_ref[_R_MV:_R_OG, :] + bt_ref[_R_MV:_R_OG, :]).astype(BF16)
        ogt_ref[c] = jax.nn.sigmoid(rt_ref[_R_OG:_R_DV, :] + bt_ref[_R_OG:_R_DV, :]).astype(BF16)
        dvt_ref[c] = (rt_ref[_R_DV:_R_GT, :] + bt_ref[_R_DV:_R_GT, :]).astype(BF16)
        gt_ref[c] = rt_ref[_R_GT:_WT_ROWS, :] + bt_ref[_R_GT:_WT_ROWS, :]

        if c + 1 < n_sub:
            xb = x_ref[(c + 1) * SLAB:(c + 2) * SLAB, :].astype(BF16)
            proj_to(pqk_ref.at[halo:halo + SLAB], xb, _C_QK)

        gate_ref[rs, 0:d] = jax.nn.sigmoid(pg1_ref[...] + bias(_C_GATE, d)).astype(BF16)
        gate_ref[rs, d:2 * d] = jax.nn.sigmoid(pg2_ref[...] + bias(_C_GATE + d, d)).astype(BF16)


def _in_proj(x2, pos2, w_packed, b_packed, conv_w, conv_b, invf, w_t, b_t, seq):
    t, d = x2.shape
    tm = ROW_TILE
    row_blk = lambda w: pl.BlockSpec((tm, w), lambda i: (i, 0))
    slab_blk = lambda r: pl.BlockSpec((tm // SLAB, r, SLAB), lambda i: (i, 0, 0))
    row_out = lambda w, dt: jax.ShapeDtypeStruct((t, w), dt)
    slab_out = lambda r, dt: jax.ShapeDtypeStruct((t // SLAB, r, SLAB), dt)
    return pl.pallas_call(
        functools.partial(_in_proj_kernel, tiles_per_seq=seq // tm),
        grid=(t // tm,),
        in_specs=[row_blk(d), pl.BlockSpec((tm // ROPE_PACK, V7X_LANES), lambda i: (i, 0))]
        + [_whole_vmem()] * 7,
        out_specs=[row_blk(d), row_blk(d), row_blk(d), row_blk(2 * d), row_blk(V7X_LANES),
                   slab_blk(_MVT_ROWS), slab_blk(d), slab_blk(_DVT_ROWS), slab_blk(_GT_ROWS)],
        out_shape=[row_out(d, BF16), row_out(d, BF16), row_out(d, BF16), row_out(2 * d, BF16),
                   row_out(V7X_LANES, F32), slab_out(_MVT_ROWS, BF16), slab_out(d, BF16),
                   slab_out(_DVT_ROWS, BF16), slab_out(_GT_ROWS, F32)],
        scratch_shapes=[pltpu.VMEM((V7X_SUBLANES + SLAB, d), F32)]
        + [pltpu.VMEM((SLAB, d), F32)] * 4 + [pltpu.VMEM((_WT_ROWS, SLAB), F32)],
        compiler_params=_params(1),
        name="in_proj",
    )(x2, pos2, w_packed, b_packed, conv_w, conv_b, invf, w_t, b_t)


def _split3_bf16(a):
    hi = a.astype(BF16)
    r1 = a - hi.astype(F32)
    mid = r1.astype(BF16)
    lo = (r1 - mid.astype(F32)).astype(BF16)
    return hi, mid, lo


def _mlstm_kernel(qk_ref, vt_ref, ogt_ref, gif_ref, gt_ref, nw_ref, out_ref,
                  c_ref, m_ref, qkt_ref, cq_ref, bcr_ref, z_ref):
    seq = qk_ref.shape[0]
    L = ML_CHUNK
    vrows = ML_DV + ML_ONES_ROWS
    c_ref[...] = jnp.zeros_like(c_ref)
    m_ref[...] = jnp.zeros_like(m_ref)

    si = lax.broadcasted_iota(jnp.int32, (L, L), 0)
    ti = lax.broadcasted_iota(jnp.int32, (L, L), 1)
    tri_lo = jnp.where(ti <= si, 1.0, 0.0).astype(BF16)
    tri_up = jnp.where(si <= ti, 1.0, 0.0).astype(BF16)
    causal_t = si <= ti
    nw = jnp.concatenate([nw_ref[...]] * (L // V7X_LANES), axis=1)

    n_chunks = seq // L

    def rows_of(c):
        return pl.ds(pl.multiple_of(c * L, L), L)

    def gate_sums(c):
        r3 = _split3_bf16(jax.nn.log_sigmoid(gt_ref[c]))
        bcr_ref[c % 2] = _dot(r3[0], tri_up) + _dot(r3[1], tri_up) + _dot(r3[2], tri_up)
        g = gif_ref[rows_of(c), :]
        c3 = _split3_bf16(jax.nn.log_sigmoid(g))
        bc_col = _dot(tri_lo, c3[0]) + _dot(tri_lo, c3[1]) + _dot(tri_lo, c3[2])
        z_ref[c % 2] = g - pltpu.roll(bc_col, V7X_LANES - ML_HEADS, 1)

    def key_query(c, h):
        q = qk_ref[rows_of(c), h * ML_DQK:(h + 1) * ML_DQK]
        k = qk_ref[rows_of(c), (ML_HEADS + h) * ML_DQK:(ML_HEADS + h + 1) * ML_DQK]
        qkt_ref[h] = _dot_nt(k, q)

    gate_sums(0)
    for h in range(ML_HEADS):
        key_query(0, h)

    def chunk(c, carry):
        rows = rows_of(c)
        nxt = jnp.minimum(c + 1, n_chunks - 1)
        gt = gt_ref[c]
        bc_row = bcr_ref[c % 2]
        z = z_ref[c % 2]
        for h in range(ML_HEADS):
            q = qk_ref[rows, h * ML_DQK:(h + 1) * ML_DQK]
            cq_ref[h] = _dot_nt(c_ref[h].astype(BF16), q)
        gate_sums(nxt)
        for h in range(ML_HEADS):
            b_row = bc_row[ML_HEADS + h:ML_HEADS + h + 1, :]
            i_row = gt[h:h + 1, :]
            r_col = z[:, h:h + 1]
            m_prev = m_ref[h]
            dmat = jnp.where(causal_t, b_row + r_col, -jnp.inf)
            inter = b_row + m_prev
            m_row = jnp.maximum(inter, jnp.max(dmat, axis=0, keepdims=True))
            w_inter = jnp.exp(inter - m_row)
            pt = (qkt_ref[h] * jnp.exp(dmat - m_row)).astype(BF16)
            key_query(nxt, h)
            vt = vt_ref[c, h * vrows:(h + 1) * vrows, :]
            num = w_inter * cq_ref[h] + _dot(vt, pt)
            den = num[ML_DV:ML_DV + 1, :]
            rden = 1.0 / jnp.maximum(jnp.abs(den), jnp.exp(-m_row))
            b_last = b_row[:, L - 1:L]
            g_row = b_last - b_row + i_row
            m_new = jnp.maximum(b_last + m_prev, jnp.max(g_row, axis=1, keepdims=True))
            decay = jnp.exp(b_last + m_prev - m_new)
            ws = jnp.exp(g_row - m_new)
            k = qk_ref[rows, (ML_HEADS + h) * ML_DQK:(ML_HEADS + h + 1) * ML_DQK]
            c_ref[h] = decay * c_ref[h] + _dot((vt.astype(F32) * ws).astype(BF16), k)
            m_ref[h] = m_new
            xn = num[0:ML_DV, :]
            xc = xn - jnp.mean(xn, axis=0, keepdims=True)
            var = jnp.mean(xc * xc, axis=0, keepdims=True)
            scale = rden * lax.rsqrt(var * rden * rden + LN_EPS)
            hn = xc * scale * nw[h * ML_DV:(h + 1) * ML_DV, :]
            gate = ogt_ref[c, h * ML_DV:(h + 1) * ML_DV, :].astype(F32)
            out_ref[rows, h * ML_DV:(h + 1) * ML_DV] = (gate * hn).T.astype(BF16)
        return carry

    lax.fori_loop(0, seq // L, chunk, 0)


def _mlstm(qk, mvt, ogt, gif, gt, nw_lanes, batch, seq):
    t, d = qk.shape
    L = ML_CHUNK
    nc = seq // L
    vrows = ML_DV + ML_ONES_ROWS
    seq_blk = lambda w: pl.BlockSpec((seq, w), lambda b: (b, 0))
    slab_blk = lambda r: pl.BlockSpec((nc, r, L), lambda b: (b, 0, 0))
    return pl.pallas_call(
        _mlstm_kernel,
        grid=(batch,),
        in_specs=[seq_blk(d), slab_blk(_MVT_ROWS), slab_blk(d), seq_blk(V7X_LANES),
                  slab_blk(_GT_ROWS), _whole_vmem()],
        out_specs=seq_blk(d),
        out_shape=jax.ShapeDtypeStruct((t, d), BF16),
        scratch_shapes=[pltpu.VMEM((ML_HEADS, vrows, ML_DQK), F32),
                        pltpu.VMEM((ML_HEADS, 1, 1), F32),
                        pltpu.VMEM((ML_HEADS, L, L), F32),
                        pltpu.VMEM((ML_HEADS, vrows, L), F32),
                        pltpu.VMEM((2, _GT_ROWS, L), F32),
                        pltpu.VMEM((2, L, V7X_LANES), F32)],
        compiler_params=_params(1),
        name="mlstm",
    )(qk, mvt, ogt, gif, gt, nw_lanes)


def _diff_attn_kernel(lq1_ref, lk1_ref, lq2_ref, lk2_ref, q_ref, k_ref, vt_ref, nw_ref,
                      o_ref, q12_ref, st_ref, mx_ref, m_ref, acc_ref, *, lam_init):
    tq = tk = ATT_TILE
    groups = q_ref.shape[0] // tq
    heads = q_ref.shape[1] // DA_DV
    n_slots = groups * heads
    i = pl.program_id(2)
    lam = (jnp.exp(jnp.sum(lq1_ref[...] * lk1_ref[...], axis=1, keepdims=True))
           - jnp.exp(jnp.sum(lq2_ref[...] * lk2_ref[...], axis=1, keepdims=True)) + lam_init)

    feat = lax.broadcasted_iota(jnp.int32, (DA_DV, tq), 0)
    first = (feat & (DA_DHEAD // 2)) == 0
    for s in range(n_slots):
        g, h = divmod(s, heads)
        qt = q_ref[g * tq:(g + 1) * tq, h * DA_DV:(h + 1) * DA_DV].astype(F32).T
        q12_ref[s, :, 0:tq] = jnp.where(first, qt, 0.0).astype(BF16)
        q12_ref[s, :, tq:2 * tq] = jnp.where(first, 0.0, qt).astype(BF16)

    vrows = DA_DV + ATT_ONES_ROWS
    n_st = st_ref.shape[0]
    ahead = ATT_SCORE_LOOKAHEAD
    assert ahead < n_st and n_slots % n_st == 0

    def scores(pos, j, s, mask_add=None):
        h = s % heads
        rows = pl.ds(pl.multiple_of(j * tk, tk), tk)
        st = _dot(k_ref[rows, h * DA_DV:(h + 1) * DA_DV], q12_ref[s])
        if mask_add is not None:
            st = st + mask_add
        st_ref[pos % n_st] = st
        mx_ref[pos % n_st] = jnp.max(st, axis=0, keepdims=True)

    def softmax(pos, s, mask_add):
        st = st_ref[pos % n_st]
        if mask_add is None:
            mx = mx_ref[pos % n_st]
        else:
            st = st + mask_add
            mx = jnp.max(st, axis=0, keepdims=True)
        m_prev = m_ref[s]
        m_new = jnp.maximum(m_prev, mx)
        alpha = jnp.exp2(m_prev - m_new)
        pt = jnp.exp2(st - m_new)
        m_ref[s] = m_new
        return alpha, pt.astype(BF16)

    def accumulate(j, s, alpha, pt):
        h = s % heads
        acc_ref[s] = alpha * acc_ref[s] + _dot(vt_ref[j, h * vrows:(h + 1) * vrows, :], pt)

    base = groups * i
    tail = [(u, s, s // heads == u) for u in range(groups)
            for s in range(n_slots) if s // heads >= u]

    for pos in range(ahead):
        scores(pos, 0, pos)
    m_ref[...] = jnp.full_like(m_ref, -jnp.inf)
    acc_ref[...] = jnp.zeros_like(acc_ref)

    def body(j, carry):
        for s in range(n_slots):
            alpha, pt = softmax(s, s, None)
            if s + ahead < n_slots:
                scores(s + ahead, j, s + ahead)
            else:
                scores(s + ahead, j + 1, s + ahead - n_slots)
            accumulate(j, s, alpha, pt)
        return carry

    lax.fori_loop(0, base, body, 0)
    kj = lax.broadcasted_iota(jnp.int32, (tk, 2 * tq), 0)
    qi = lax.broadcasted_iota(jnp.int32, (tk, 2 * tq), 1) & (tq - 1)
    causal_add = jnp.where(kj <= qi, 0.0, -jnp.inf)
    for pos, (u, s, diagonal) in enumerate(tail):
        alpha, pt = softmax(pos, s, causal_add if diagonal and pos < ahead else None)
        if pos + ahead < len(tail):
            u2, s2, diagonal2 = tail[pos + ahead]
            scores(pos + ahead, base + u2, s2, causal_add if diagonal2 else None)
        accumulate(base + u, s, alpha, pt)

    for s in range(n_slots):
        g, h = divmod(s, heads)
        acc = acc_ref[s, 0:DA_DV, :]
        r = 1.0 / acc_ref[s, DA_DV:DA_DV + 1, :]
        ot = acc[:, 0:tq] * r[:, 0:tq] - acc[:, tq:2 * tq] * (lam * r[:, tq:2 * tq])
        ms = jnp.mean(ot * ot, axis=0, keepdims=True)
        o = (ot * (lax.rsqrt(ms + LN_EPS) * (1.0 - lam_init))).T
        o_ref[g * tq:(g + 1) * tq, h * DA_DV:(h + 1) * DA_DV] = (
            o * nw_ref[:, h * DA_DV:(h + 1) * DA_DV]).astype(BF16)


def _diff_attn(dq, dk, dvt, lam_vecs, da_norm_w, batch, seq, lam_init):
    t, d = dq.shape
    tq = ATT_TILE
    assert tq & (tq - 1) == 0
    nq = seq // tq
    hb = ATT_HEADS_PER_STEP
    qt = ATT_Q_TILES_PER_STEP
    assert nq % qt == 0
    steps = nq // qt
    slots = qt * hb
    w = hb * DA_DV
    vrows = DA_DV + ATT_ONES_ROWS
    lam_spec = pl.BlockSpec((1, DA_DHEAD), lambda b, g, i: (0, 0))
    q_spec = pl.BlockSpec((qt * tq, w), lambda b, g, i: (b * steps + i, g))
    return pl.pallas_call(
        functools.partial(_diff_attn_kernel, lam_init=lam_init),
        grid=(batch, DA_HEADS // hb, steps),
        in_specs=[lam_spec] * 4 + [q_spec,
                                   pl.BlockSpec((seq, w), lambda b, g, i: (b, g)),
                                   pl.BlockSpec((nq, hb * vrows, tq), lambda b, g, i: (b, g, 0)),
                                   pl.BlockSpec((1, w), lambda b, g, i: (0, g))],
        out_specs=q_spec,
        out_shape=jax.ShapeDtypeStruct((t, d), BF16),
        scratch_shapes=[pltpu.VMEM((slots, DA_DV, 2 * tq), BF16),
                        pltpu.VMEM((ATT_SCORE_BUFFERS, tq, 2 * tq), F32),
                        pltpu.VMEM((ATT_SCORE_BUFFERS, 1, 2 * tq), F32),
                        pltpu.VMEM((slots, 1, 2 * tq), F32),
                        pltpu.VMEM((slots, vrows, 2 * tq), F32)],
        compiler_params=_params(3),
        name="diff_attn",
    )(*lam_vecs, dq, dk, dvt, da_norm_w)


def _row_subs(tm):
    return [slice(r * ROW_SUB, (r + 1) * ROW_SUB) for r in range(tm // ROW_SUB)]


def _pipeline_subs(subs, first_matmuls, middle, second_matmul, finish):
    a = first_matmuls(subs[0])
    c_prev = None
    for r, s in enumerate(subs):
        a_next = first_matmuls(subs[r + 1]) if r + 1 < len(subs) else None
        c = second_matmul(middle(s, a))
        if c_prev is not None:
            finish(subs[r - 1], c_prev)
        a, c_prev = a_next, c
    finish(subs[-1], c_prev)


def _merge_kernel(x_ref, ha_ref, ob_ref, gate_ref, wa_ref, wb_ref, wm_ref, g_ref, b_ref, o_ref):
    d = x_ref.shape[1]

    def branch_projections(s):
        return _dot(ha_ref[s, :], wa_ref[...]), _dot(ob_ref[s, :], wb_ref[...])

    def gated_merge(s, yab):
        ya, yb = yab
        mixin = gate_ref[s, 0:d].astype(F32) * ya + gate_ref[s, d:2 * d].astype(F32) * yb
        return mixin.astype(BF16)

    def finish(s, mix):
        o_ref[s, :] = _layer_norm(ALPHA * x_ref[s, :] + mix, g_ref[...], b_ref[...])

    _pipeline_subs(_row_subs(x_ref.shape[0]), branch_projections, gated_merge,
                   lambda mixin: _dot(mixin, wm_ref[...]), finish)


def _merge(x2, ha, ob, gates, wa, wb, wm, g, b):
    t, d = x2.shape
    tm = ROW_TILE_WIDE
    row_blk = lambda w: pl.BlockSpec((tm, w), lambda i: (i, 0))
    return pl.pallas_call(
        _merge_kernel,
        grid=(t // tm,),
        in_specs=[row_blk(d), row_blk(d), row_blk(d), row_blk(2 * d)] + [_whole_vmem()] * 5,
        out_specs=row_blk(d),
        out_shape=jax.ShapeDtypeStruct((t, d), F32),
        compiler_params=_params(1),
        name="merge_ln1",
    )(x2, ha, ob, gates, wa, wb, wm, g, b)


def _xattn_kv_kernel(mem_ref, wk_ref, wv_ref, k_ref, v_ref):
    mb = mem_ref[...].astype(BF16)
    k_ref[...] = _dot(mb, wk_ref[...]).astype(BF16)
    v_ref[...] = _dot(mb, wv_ref[...]).astype(BF16)


def _xattn_kv(mem2, wk, wv):
    t, d = mem2.shape
    rows = math.gcd(t, ROW_TILE_WIDE)
    blk = pl.BlockSpec((rows, d), lambda b: (b, 0))
    return pl.pallas_call(
        _xattn_kv_kernel,
        grid=(t // rows,),
        in_specs=[blk, _whole_vmem(), _whole_vmem()],
        out_specs=[blk, blk],
        out_shape=[jax.ShapeDtypeStruct((t, d), BF16)] * 2,
        compiler_params=_params(1),
        name="xattn_kv",
    )(mem2, wk, wv)


def _xattn_kernel(x_ref, k_ref, v_ref, wq_ref, wo_ref, g_ref, b_ref, o_ref):
    d = x_ref.shape[1]
    dh = d // XA_HEADS
    subs = _row_subs(x_ref.shape[0])
    head_cols = [slice(h * dh, (h + 1) * dh) for h in range(XA_HEADS)]

    def project_q(s):
        return (_dot(x_ref[s, :].astype(BF16), wq_ref[...]) * (dh ** -0.5)).astype(BF16)

    def scores(q):
        return [_dot_nt(q[:, c], k_ref[:, c]) for c in head_cols]

    def softmax(sc):
        e = jnp.exp(sc - jnp.max(sc, axis=1, keepdims=True))
        return (e / jnp.sum(e, axis=1, keepdims=True)).astype(BF16)

    def finish(r, xo):
        o_ref[subs[r], :] = _layer_norm(ALPHA * x_ref[subs[r], :] + xo, g_ref[...], b_ref[...])

    n = len(subs)
    q = project_q(subs[0])
    sc = scores(q)
    xo_prev = None
    for r in range(n):
        if r + 1 < n:
            q = project_q(subs[r + 1])
        ps = [softmax(s_h) for s_h in sc]
        o = jnp.concatenate([_dot(p, v_ref[:, c]).astype(BF16) for p, c in zip(ps, head_cols)],
                            axis=1)
        if r + 1 < n:
            sc = scores(q)
        xo = _dot(o, wo_ref[...])
        if xo_prev is not None:
            finish(r - 1, xo_prev)
        xo_prev = xo
    finish(n - 1, xo_prev)


def _xattn(x1, kx, vx, wq, wo, g, b, seq, mem_len):
    t, d = x1.shape
    tm = ROW_TILE_WIDE
    tiles_per_seq = seq // tm
    row_blk = pl.BlockSpec((tm, d), lambda i: (i, 0))
    kv_blk = pl.BlockSpec((mem_len, d), lambda i: (i // tiles_per_seq, 0))
    return pl.pallas_call(
        _xattn_kernel,
        grid=(t // tm,),
        in_specs=[row_blk, kv_blk, kv_blk] + [_whole_vmem()] * 4,
        out_specs=row_blk,
        out_shape=jax.ShapeDtypeStruct((t, d), F32),
        compiler_params=_params(1),
        name="xattn_ln2",
    )(x1, kx, vx, wq, wo, g, b)


def _ffn_kernel(x_ref, wg_ref, wu_ref, wd_ref, g_ref, b_ref, o_ref):
    def gate_up(s):
        xb = x_ref[s, :].astype(BF16)
        return _dot(xb, wg_ref[...]), _dot(xb, wu_ref[...])

    def hidden(s, gu):
        gt, up = gu
        return (gt * jax.nn.sigmoid(gt) * up).astype(BF16)

    def finish(s, dn):
        o_ref[s, :] = _layer_norm(ALPHA * x_ref[s, :] + dn, g_ref[...], b_ref[...])

    _pipeline_subs(_row_subs(x_ref.shape[0]), gate_up, hidden,
                   lambda hid: _dot(hid, wd_ref[...]), finish)


def _ffn(x2, wg, wu, wd, g, b):
    t, d = x2.shape
    tm = ROW_TILE_WIDE
    row_blk = pl.BlockSpec((tm, d), lambda i: (i, 0))
    return pl.pallas_call(
        _ffn_kernel,
        grid=(t // tm,),
        in_specs=[row_blk] + [_whole_vmem()] * 5,
        out_specs=row_blk,
        out_shape=jax.ShapeDtypeStruct((t, d), F32),
        compiler_params=_params(1),
        name="ffn_ln3",
    )(x2, wg, wu, wd, g, b)


def _rope_lane_order(a):
    lead = a.shape[:-1]
    a = a.reshape(lead + (DA_HEADS, 2, 2, DA_DHEAD // 2))
    return jnp.swapaxes(a, -3, -2).reshape(lead + (-1,))


def kernel(x, mem, positions, w_in, b_in, conv_w, conv_b, ml_norm_w, lam_q1, lam_k1, lam_q2, lam_k2, da_norm_w, w_proj_a, w_proj_b, w_mix_out, ln1_g, ln1_b, w_xq, w_xk, w_xv, w_xo, ln2_g, ln2_b, w_ffn_gate, w_ffn_up, w_ffn_down, ln3_g, ln3_b):
    batch, seq, d = x.shape
    mem_len = mem.shape[1]
    depth = w_in.shape[0]
    assert depth == DEPTH and d == D_MODEL
    assert seq % ROW_TILE == 0 and seq % ROW_TILE_WIDE == 0
    assert seq % ML_CHUNK == 0 and seq % ATT_TILE == 0
    t = batch * seq

    ml_qk_w = 2 * ML_HEADS * ML_DQK
    ml_v_w = ML_HEADS * ML_DV
    da_w = DA_HEADS * DA_DV
    sizes = (ml_qk_w, ml_v_w, ml_v_w, 2 * ML_HEADS, da_w, da_w, da_w, 2 * d)
    offs = np.concatenate([[0], np.cumsum(sizes)])
    assert ml_qk_w == d and ml_v_w == d and da_w == d and offs[-1] == w_in.shape[2]
    half = DA_DHEAD // 2
    invf = ROPE_THETA ** (-jnp.arange(half, dtype=F32) / half)
    invf = jnp.tile(invf, V7X_LANES // half).reshape(1, V7X_LANES)

    x2 = x.reshape(t, d)
    pos2 = jnp.repeat(positions.reshape(t // SLAB, ROPE_PACK, SLAB // ROPE_PACK).transpose(0, 2, 1),
                      V7X_LANES // ROPE_PACK, axis=2).reshape(t // ROPE_PACK, V7X_LANES)
    mem2 = mem.reshape(batch * mem_len, d)

    for l in range(depth):
        def piece(a, j):
            return a[..., offs[j]:offs[j + 1]]

        def pack(a):
            pad = jnp.zeros(a.shape[:-1] + (V7X_LANES - 2 * ML_HEADS,), a.dtype)
            return jnp.concatenate(
                [piece(a, 0), _rope_lane_order(piece(a, 4)), _rope_lane_order(piece(a, 5)),
                 piece(a, 7), piece(a, 3), pad], axis=-1)

        w_bf = w_in[l].astype(BF16)
        w_packed = pack(w_bf)
        b_packed = pack(b_in[l].reshape(1, -1))
        assert w_packed.shape[1] == _W_PACKED

        def with_ones(j, heads, dv, ones):
            wt = jnp.pad(piece(w_bf, j).T.reshape(heads, dv, d), ((0, 0), (0, ones), (0, 0)))
            bt = jnp.pad(piece(b_in[l], j).reshape(heads, dv), ((0, 0), (0, ones)),
                         constant_values=1.0)
            return wt.reshape(-1, d), bt.reshape(-1)

        w_mvt, b_mvt = with_ones(1, ML_HEADS, ML_DV, ML_ONES_ROWS)
        w_dvt, b_dvt = with_ones(6, DA_HEADS, DA_DV, ATT_ONES_ROWS)
        gpad = _GT_ROWS - 2 * ML_HEADS
        w_t = jnp.concatenate([w_mvt, piece(w_bf, 2).T, w_dvt,
                               jnp.pad(piece(w_bf, 3).T, ((0, gpad), (0, 0)))], axis=0)
        b_t = jnp.concatenate([b_mvt, piece(b_in[l], 2), b_dvt,
                               jnp.pad(piece(b_in[l], 3), (0, gpad))]).reshape(-1, 1)
        assert w_t.shape[0] == _WT_ROWS

        qk, dq, dk, gates, gif, mvt, ogt, dvt, gt = _in_proj(
            x2, pos2, w_packed, b_packed, conv_w[l], conv_b[l].reshape(1, -1), invf,
            w_t, b_t, seq)

        nw_lanes = jnp.broadcast_to(ml_norm_w[l].reshape(-1, 1), (ML_HEADS * ML_DV, V7X_LANES))
        ha = _mlstm(qk, mvt, ogt, gif, gt, nw_lanes, batch, seq)

        lam_init = 0.8 - 0.6 * math.exp(-0.3 * l)
        lam_vecs = [a[l].reshape(1, -1) for a in (lam_q1, lam_k1, lam_q2, lam_k2)]
        ob = _diff_attn(dq, dk, dvt, lam_vecs, da_norm_w[l].reshape(1, -1), batch, seq, lam_init)

        x2 = _merge(x2, ha, ob, gates, w_proj_a[l].astype(BF16), w_proj_b[l].astype(BF16),
                    w_mix_out[l].astype(BF16), ln1_g[l].reshape(1, -1), ln1_b[l].reshape(1, -1))

        kx, vx = _xattn_kv(mem2, w_xk[l].astype(BF16), w_xv[l].astype(BF16))
        x2 = _xattn(x2, kx, vx, w_xq[l].astype(BF16), w_xo[l].astype(BF16),
                    ln2_g[l].reshape(1, -1), ln2_b[l].reshape(1, -1), seq, mem_len)

        x2 = _ffn(x2, w_ffn_gate[l].astype(BF16), w_ffn_up[l].astype(BF16),
                  w_ffn_down[l].astype(BF16), ln3_g[l].reshape(1, -1), ln3_b[l].reshape(1, -1))

    return x2.reshape(batch, seq, d)
```
